```python
import jax, jax.numpy as jnp
from jax import lax
import numpy as np


D_MODEL = 4096
BATCH = 1
SEQ = 16384
DEPTH = 4
DEC_BATCH = 1
DEC_SEQ = 8192
PAST_LEN = 128

GRID_W = 64
NA_HEADS = 16
NA_HEAD_DIM = 128
NA_WIDTH = NA_HEADS * NA_HEAD_DIM
NA_ROWS_MAX = 8
NA_COLS = 16
ML_HEADS = 8
ML_QK_DIM = 128
ML_V_DIM = 256
ML_QK_WIDTH = ML_HEADS * ML_QK_DIM
ML_V_WIDTH = ML_HEADS * ML_V_DIM
ML_CONV = 5
ML_CHUNK = 64
N_EXPERTS = 16
EC_CAPACITY = 2
D_FF = 2048
EPS = 1e-6

OFF_NA_Q = 0
OFF_NA_K = OFF_NA_Q + NA_WIDTH
OFF_NA_V = OFF_NA_K + NA_WIDTH
OFF_ML_Q = OFF_NA_V + NA_WIDTH
OFF_ML_K = OFF_ML_Q + ML_QK_WIDTH
OFF_ML_V = OFF_ML_K + ML_QK_WIDTH
OFF_ML_O = OFF_ML_V + ML_V_WIDTH
OFF_ML_G = OFF_ML_O + ML_V_WIDTH
OFF_MERGE = OFF_ML_G + 4 * ML_HEADS
IN_COLS = OFF_MERGE + 2 * D_MODEL

kernel_name = "hybrid_natten_mlstm_ec_encoder"


def rmsnorm(x, g):
    xf = x.astype(jnp.float32)
    y = xf * lax.rsqrt(jnp.mean(xf * xf, axis=-1, keepdims=True) + EPS)
    return (y * g.astype(jnp.float32)).astype(x.dtype)


def neighborhood_attention(q, k, v, rpb):
    B, S, H, Dh = q.shape
    rows = S // GRID_W
    kr = min(NA_ROWS_MAX, rows)
    cols = np.arange(GRID_W)
    col0 = np.clip(cols - NA_COLS // 2, 0, GRID_W - NA_COLS)
    col_idx = col0[:, None] + np.arange(NA_COLS)[None, :]
    dc_idx = col_idx - cols[:, None] + (NA_COLS - 1)
    rpb_c = rpb[:, :, dc_idx]
    qg = q.reshape(B, rows, GRID_W, H, Dh)
    kg = k.reshape(B, rows, GRID_W, H, Dh)
    vg = v.reshape(B, rows, GRID_W, H, Dh)

    def row_block(r):
        r0 = jnp.clip(r - kr // 2, 0, rows - kr)
        k_band = lax.dynamic_slice_in_dim(kg, r0, kr, axis=1)
        v_band = lax.dynamic_slice_in_dim(vg, r0, kr, axis=1)
        k_nb = k_band[:, :, col_idx]
        v_nb = v_band[:, :, col_idx]
        q_r = lax.dynamic_index_in_dim(qg, r, axis=1, keepdims=False)
        s = jnp.einsum('bwhd,bkwjhd->bhwkj', q_r, k_nb).astype(jnp.float32)
        dr_idx = r0 + jnp.arange(kr) - r + (NA_ROWS_MAX - 1)
        bias = jnp.transpose(rpb_c[:, dr_idx], (0, 2, 1, 3))
        s = s + bias[None].astype(jnp.float32)
        p = jax.nn.softmax(s.reshape(B, H, GRID_W, kr * NA_COLS), axis=-1)
        p = p.reshape(B, H, GRID_W, kr, NA_COLS).astype(v.dtype)
        return jnp.einsum('bhwkj,bkwjhd->bwhd', p, v_nb)

    out = lax.map(row_block, jnp.arange(rows))
    return jnp.transpose(out, (1, 0, 2, 3, 4)).reshape(B, S, H * Dh)


def mlstm_chunkwise(q, k, v, log_i, log_f):
    B, H, S, Dk = q.shape
    Dv = v.shape[-1]
    L = ML_CHUNK
    nc = S // L
    f32 = jnp.float32

    def to_chunks(a):
        return jnp.moveaxis(a.reshape(B, H, nc, L, *a.shape[3:]), 2, 0)

    qc, kc, vc = to_chunks(q.astype(f32)), to_chunks(k.astype(f32)), to_chunks(v.astype(f32))
    ic, fc = to_chunks(log_i), to_chunks(log_f)
    lower = jnp.tril(jnp.ones((L, L), dtype=bool))

    def step(carry, inp):
        C, n, m = carry
        qb, kb, vb, ib, fb = inp
        b = jnp.cumsum(fb, axis=-1)
        d = jnp.where(lower, b[..., :, None] - b[..., None, :] + ib[..., None, :], -jnp.inf)
        inter = b + m[..., None]
        m_t = jnp.maximum(inter, jnp.max(d, axis=-1))
        s = jnp.einsum('bhtd,bhsd->bhts', qb, kb) * jnp.exp(d - m_t[..., None])
        sc = jnp.exp(inter - m_t)
        num = jnp.einsum('bhts,bhsv->bhtv', s, vb) + sc[..., None] * jnp.einsum('bhtd,bhdv->bhtv', qb, C)
        den = jnp.sum(s, axis=-1) + sc * jnp.einsum('bhtd,bhd->bht', qb, n)
        h = num / jnp.maximum(jnp.abs(den), jnp.exp(-m_t))[..., None]
        b_end = b[..., -1]
        d_end = b_end[..., None] - b + ib
        m_new = jnp.maximum(b_end + m, jnp.max(d_end, axis=-1))
        wk = jnp.exp(d_end - m_new[..., None])
        decay = jnp.exp(b_end + m - m_new)
        C_new = decay[..., None, None] * C + jnp.einsum('bhs,bhsd,bhsv->bhdv', wk, kb, vb)
        n_new = decay[..., None] * n + jnp.einsum('bhs,bhsd->bhd', wk, kb)
        return (C_new, n_new, m_new), h

    init = (jnp.zeros((B, H, Dk, Dv), f32), jnp.zeros((B, H, Dk), f32), jnp.zeros((B, H), f32))
    _, h = lax.scan(step, init, (qc, kc, vc, ic, fc))
    return jnp.moveaxis(h, 0, 2).reshape(B, H, S, Dv)


def centred_depthwise_conv(x, w, b):
    C = x.shape[-1]
    y = lax.conv_general_dilated(x, w[:, None, :].astype(x.dtype), window_strides=(1,),
                                 padding=[(ML_CONV // 2, ML_CONV // 2)],
                                 dimension_numbers=('NWC', 'WIO', 'NWC'), feature_group_count=C)
    return y + b.astype(x.dtype)


def token_mixer(xn, w_in, b_gate, conv_w, conv_b, q_gain, k_gain, rpb, ml_gain, w_pa, w_pm, w_out):
    B, S, _ = xn.shape
    z = xn @ w_in
    q = rmsnorm(z[..., OFF_NA_Q:OFF_NA_K].reshape(B, S, NA_HEADS, NA_HEAD_DIM), q_gain) * (NA_HEAD_DIM ** -0.5)
    k = rmsnorm(z[..., OFF_NA_K:OFF_NA_V].reshape(B, S, NA_HEADS, NA_HEAD_DIM), k_gain)
    v = z[..., OFF_NA_V:OFF_ML_Q].reshape(B, S, NA_HEADS, NA_HEAD_DIM)
    y_a = neighborhood_attention(q, k, v, rpb)
    qk = jax.nn.silu(centred_depthwise_conv(z[..., OFF_ML_Q:OFF_ML_V], conv_w, conv_b))
    mq = jnp.transpose(qk[..., :ML_QK_WIDTH].reshape(B, S, ML_HEADS, ML_QK_DIM), (0, 2, 1, 3)) * (ML_QK_DIM ** -0.5)
    mk = jnp.transpose(qk[..., ML_QK_WIDTH:].reshape(B, S, ML_HEADS, ML_QK_DIM), (0, 2, 1, 3))
    mv = jnp.transpose(z[..., OFF_ML_V:OFF_ML_O].reshape(B, S, ML_HEADS, ML_V_DIM), (0, 2, 1, 3))
    og = z[..., OFF_ML_O:OFF_ML_G]
    gates = (z[..., OFF_ML_G:OFF_MERGE] + b_gate).astype(jnp.float32)
    gates = jnp.transpose(gates.reshape(B, S, 4, ML_HEADS), (2, 0, 3, 1))
    h_fwd = mlstm_chunkwise(mq, mk, mv, gates[0], jax.nn.log_sigmoid(gates[1]))
    flip = lambda a: jnp.flip(a, axis=2)
    h_bwd = flip(mlstm_chunkwise(flip(mq), flip(mk), flip(mv), flip(gates[2]), flip(jax.nn.log_sigmoid(gates[3]))))
    h = jnp.transpose(h_fwd + h_bwd, (0, 2, 1, 3))
    h = rmsnorm(h, ml_gain).reshape(B, S, ML_V_WIDTH).astype(xn.dtype)
    y_m = h * jax.nn.sigmoid(og)
    g = jax.nn.sigmoid(z[..., OFF_MERGE:])
    merged = g[..., :D_MODEL] * (y_a @ w_pa) + g[..., D_MODEL:] * (y_m @ w_pm)
    return merged @ w_out


def expert_choice_ffn(xn, w_router, w_ff_gate, w_ff_up, w_ff_down):
    B, S, D = xn.shape
    T = B * S
    cap = EC_CAPACITY * T // N_EXPERTS
    xt = xn.reshape(T, D)
    aff = jax.nn.softmax((xt @ w_router).astype(jnp.float32), axis=-1)
    gate, idx = lax.top_k(aff.T, cap)
    xe = xt[idx]
    hid = jax.nn.silu(jnp.einsum('ecd,edf->ecf', xe, w_ff_gate)) * jnp.einsum('ecd,edf->ecf', xe, w_ff_up)
    ye = jnp.einsum('ecf,efd->ecd', hid, w_ff_down) * gate[..., None].astype(xn.dtype)
    out = jnp.zeros_like(xt).at[idx.reshape(-1)].add(ye.reshape(-1, D))
    return out.reshape(B, S, D)


def trunk(x, norm1_g, norm2_g, w_in, b_gate, conv_w, conv_b, q_gain, k_gain, rpb, ml_gain,
          w_pa, w_pm, w_out, w_router, w_ff_gate, w_ff_up, w_ff_down):
    for l in range(DEPTH):
        h = x + token_mixer(rmsnorm(x, norm1_g[l]), w_in[l], b_gate[l], conv_w[l], conv_b[l],
                            q_gain[l], k_gain[l], rpb[l], ml_gain[l], w_pa[l], w_pm[l], w_out[l])
        x = h + expert_choice_ffn(rmsnorm(h, norm2_g[l]), w_router[l], w_ff_gate[l], w_ff_up[l], w_ff_down[l])
    return x


def setup_inputs(seed: int = 0) -> dict:
    key = jax.random.key(seed)
    ks = jax.random.split(key, 20)
    f32 = jnp.float32

    def nrm(k, shape, scale):
        return jax.random.normal(k, shape, f32) * scale

    gate_offset = jnp.repeat(jnp.array([0.0, 3.0, 0.0, 3.0], f32), ML_HEADS)
    return {
        "x_prompt": nrm(ks[0], (BATCH, SEQ, D_MODEL), 1.0),
        "x_sample": nrm(ks[1], (DEC_BATCH, DEC_SEQ, D_MODEL), 1.0),
        "norm1_g": 1.0 + nrm(ks[2], (DEPTH, D_MODEL), 0.05),
        "norm2_g": 1.0 + nrm(ks[3], (DEPTH, D_MODEL), 0.05),
        "w_in": nrm(ks[4], (DEPTH, D_MODEL, IN_COLS), D_MODEL ** -0.5),
        "b_gate": gate_offset + nrm(ks[5], (DEPTH, 4 * ML_HEADS), 0.5),
        "conv_w": nrm(ks[6], (DEPTH, ML_CONV, 2 * ML_QK_WIDTH), ML_CONV ** -0.5),
        "conv_b": nrm(ks[7], (DEPTH, 2 * ML_QK_WIDTH), 0.02),
        "q_gain": 1.0 + nrm(ks[8], (DEPTH, NA_HEAD_DIM), 0.05),
        "k_gain": 1.0 + nrm(ks[9], (DEPTH, NA_HEAD_DIM), 0.05),
        "rpb": nrm(ks[10], (DEPTH, NA_HEADS, 2 * NA_ROWS_MAX - 1, 2 * NA_COLS - 1), 0.1),
        "ml_gain": 1.0 + nrm(ks[11], (DEPTH, ML_HEADS, ML_V_DIM), 0.05),
        "w_pa": nrm(ks[12], (DEPTH, NA_WIDTH, D_MODEL), NA_WIDTH ** -0.5),
        "w_pm": nrm(ks[13], (DEPTH, ML_V_WIDTH, D_MODEL), ML_V_WIDTH ** -0.5),
        "w_out": nrm(ks[14], (DEPTH, D_MODEL, D_MODEL), D_MODEL ** -0.5),
        "w_router": nrm(ks[15], (DEPTH, D_MODEL, N_EXPERTS), D_MODEL ** -0.5),
        "w_ff_gate": nrm(ks[16], (DEPTH, N_EXPERTS, D_MODEL, D_FF), D_MODEL ** -0.5),
        "w_ff_up": nrm(ks[17], (DEPTH, N_EXPERTS, D_MODEL, D_FF), D_MODEL ** -0.5),
        "w_ff_down": nrm(ks[18], (DEPTH, N_EXPERTS, D_FF, D_MODEL), D_FF ** -0.5),
    }


def reference(x_prompt, x_sample, norm1_g, norm2_g, w_in, b_gate, conv_w, conv_b, q_gain, k_gain, rpb,
              ml_gain, w_pa, w_pm, w_out, w_router, w_ff_gate, w_ff_up, w_ff_down):
    y_prompt = trunk(x_prompt, norm1_g, norm2_g, w_in, b_gate, conv_w, conv_b, q_gain, k_gain, rpb, ml_gain,
                     w_pa, w_pm, w_out, w_router, w_ff_gate, w_ff_up, w_ff_down)
    y_sample = trunk(x_sample, norm1_g, norm2_g, w_in, b_gate, conv_w, conv_b, q_gain, k_gain, rpb, ml_gain,
                     w_pa, w_pm, w_out, w_router, w_ff_gate, w_ff_up, w_ff_down)
    return (y_prompt, y_sample)
```

```python
import dataclasses
import functools

import jax
import jax.numpy as jnp
import numpy as np
from jax import lax
from jax.experimental import pallas as pl
from jax.experimental.pallas import tpu as pltpu

F32 = jnp.float32
BF16 = jnp.bfloat16
I32 = jnp.int32
NEG = -1e30
LANES = 128
VMEM_LIMIT = 56 * 1024 * 1024


@dataclasses.dataclass(frozen=True)
class Cfg:
    d_model: int = 4096
    grid_w: int = 64
    na_heads: int = 16
    na_dim: int = 128
    na_rows: int = 8
    na_cols: int = 16
    ml_heads: int = 8
    ml_qk: int = 128
    ml_v: int = 256
    ml_conv: int = 5
    n_experts: int = 16
    ec_capacity: int = 2
    d_ff: int = 2048
    eps: float = 1e-6
    ml_chunk: int = 256

    @property
    def naw(self):
        return self.na_heads * self.na_dim

    @property
    def qkw(self):
        return self.ml_heads * self.ml_qk

    @property
    def vw(self):
        return self.ml_heads * self.ml_v

    @property
    def z_na_k(self):
        return self.naw

    @property
    def z_na_v(self):
        return 2 * self.naw

    @property
    def z_ml_q(self):
        return 3 * self.naw

    @property
    def z_ml_v(self):
        return self.z_ml_q + 2 * self.qkw

    @property
    def z_ml_o(self):
        return self.z_ml_v + self.vw

    @property
    def z_merge(self):
        return self.z_ml_o + self.vw

    @property
    def z_cols(self):
        return self.z_merge + 2 * self.d_model


CFG = Cfg()


def _tile(dim, pref, mult=8):
    if dim <= pref:
        return dim
    t = (pref // mult) * mult
    while t > mult and dim % t:
        t -= mult
    assert dim % t == 0, (dim, pref, mult)
    return t


def _params(*sem):
    return pltpu.CompilerParams(dimension_semantics=sem, vmem_limit_bytes=VMEM_LIMIT)


def _sigmoid(x):
    return 1.0 / (1.0 + jnp.exp(-x))


def _norm_proj_kernel(x_ref, g_ref, w_ref, b_ref, xn_ref, s_ref, *, eps, softmax_cols):
    x = x_ref[...]
    ms = jnp.mean(x * x, axis=-1, keepdims=True)
    xn = (x * lax.rsqrt(ms + eps) * g_ref[...]).astype(BF16)
    xn_ref[...] = xn
    s = jnp.dot(xn, w_ref[...], preferred_element_type=F32) + b_ref[...]
    if softmax_cols:
        lane = lax.broadcasted_iota(I32, s.shape, 1)
        s = jnp.where(lane < softmax_cols, s, NEG)
        e = jnp.exp(s - jnp.max(s, axis=-1, keepdims=True))
        s = e / jnp.sum(e, axis=-1, keepdims=True)
    s_ref[...] = s


def norm_proj(x, g, w_small, b_small, *, eps, softmax_cols=0):
    n, d = x.shape
    tm = _tile(n, 512)
    return pl.pallas_call(
        functools.partial(_norm_proj_kernel, eps=eps, softmax_cols=softmax_cols),
        grid=(n // tm,),
        in_specs=[
            pl.BlockSpec((tm, d), lambda i: (i, 0)),
            pl.BlockSpec((1, d), lambda i: (0, 0)),
            pl.BlockSpec((d, LANES), lambda i: (0, 0)),
            pl.BlockSpec((1, LANES), lambda i: (0, 0)),
        ],
        out_specs=[
            pl.BlockSpec((tm, d), lambda i: (i, 0)),
            pl.BlockSpec((tm, LANES), lambda i: (i, 0)),
        ],
        out_shape=[jax.ShapeDtypeStruct((n, d), BF16), jax.ShapeDtypeStruct((n, LANES), F32)],
        compiler_params=_params("parallel"),
        name="norm_proj",
    )(x, g, w_small, b_small)


def _mm_kernel(x_ref, w_ref, o_ref):
    o_ref[...] = jnp.dot(x_ref[...], w_ref[...], preferred_element_type=F32).astype(o_ref.dtype)


def in_proj(xn, w_main, layer):
    n, d = xn.shape
    zc = w_main.shape[-1]
    tm, tn = _tile(n, 1024), _tile(zc, 1024, LANES)
    return pl.pallas_call(
        _mm_kernel,
        grid=(n // tm, zc // tn),
        in_specs=[
            pl.BlockSpec((tm, d), lambda i, j: (i, 0)),
            pl.BlockSpec((None, d, tn), lambda i, j: (layer, 0, j)),
        ],
        out_specs=pl.BlockSpec((tm, tn), lambda i, j: (i, j)),
        out_shape=jax.ShapeDtypeStruct((n, zc), BF16),
        compiler_params=_params("parallel", "parallel"),
        name="in_proj",
    )(xn, w_main)


def _merge_kernel(ya_ref, ym_ref, wa_ref, wm_ref, ga_ref, gm_ref, o_ref):
    a = jnp.dot(ya_ref[...], wa_ref[...], preferred_element_type=F32)
    m = jnp.dot(ym_ref[...], wm_ref[...], preferred_element_type=F32)
    o = _sigmoid(ga_ref[...].astype(F32)) * a + _sigmoid(gm_ref[...].astype(F32)) * m
    o_ref[...] = o.astype(o_ref.dtype)


def merge_proj(ya, ym, w_pa, w_pm, z, layer, cfg):
    n = ya.shape[0]
    d = cfg.d_model
    tm, tn = _tile(n, 512), _tile(d, 1024, LANES)
    assert cfg.z_merge % tn == 0
    ga0 = cfg.z_merge // tn
    gm0 = (cfg.z_merge + d) // tn
    return pl.pallas_call(
        _merge_kernel,
        grid=(n // tm, d // tn),
        in_specs=[
            pl.BlockSpec((tm, cfg.naw), lambda i, j: (i, 0)),
            pl.BlockSpec((tm, cfg.vw), lambda i, j: (i, 0)),
            pl.BlockSpec((None, cfg.naw, tn), lambda i, j: (layer, 0, j)),
            pl.BlockSpec((None, cfg.vw, tn), lambda i, j: (layer, 0, j)),
            pl.BlockSpec((tm, tn), lambda i, j: (i, ga0 + j)),
            pl.BlockSpec((tm, tn), lambda i, j: (i, gm0 + j)),
        ],
        out_specs=pl.BlockSpec((tm, tn), lambda i, j: (i, j)),
        out_shape=jax.ShapeDtypeStruct((n, d), BF16),
        compiler_params=_params("parallel", "parallel"),
        name="merge_proj",
    )(ya, ym, w_pa, w_pm, z, z)


def _out_proj_kernel(m_ref, w_ref, r_ref, o_ref):
    o_ref[...] = r_ref[...] + jnp.dot(m_ref[...], w_ref[...], preferred_element_type=F32)


def out_proj(merged, w_out, res, layer):
    n, d = merged.shape
    tm, tn = _tile(n, 512), _tile(d, 1024, LANES)
    return pl.pallas_call(
        _out_proj_kernel,
        grid=(n // tm, d // tn),
        in_specs=[
            pl.BlockSpec((tm, d), lambda i, j: (i, 0)),
            pl.BlockSpec((None, d, tn), lambda i, j: (layer, 0, j)),
            pl.BlockSpec((tm, tn), lambda i, j: (i, j)),
        ],
        out_specs=pl.BlockSpec((tm, tn), lambda i, j: (i, j)),
        out_shape=jax.ShapeDtypeStruct((n, d), F32),
        compiler_params=_params("parallel", "parallel"),
        name="out_proj",
    )(merged, w_out, res)


def na_bias_table(rpb, cfg):
    w_, r_, c_ = cfg.grid_w, cfg.na_rows, cfg.na_cols
    cols = np.arange(w_)
    col0 = np.clip(cols - c_ // 2, 0, w_ - c_)
    kc = np.arange(w_)
    valid = (kc[None, :] >= col0[:, None]) & (kc[None, :] < col0[:, None] + c_)
    dc = np.clip(kc[None, :] - cols[:, None] + (c_ - 1), 0, 2 * c_ - 2)
    dr = np.arange(r_)[None, :] - np.arange(r_)[:, None] + (r_ - 1)
    b = rpb[:, dr[:, None, :, None], dc[None, :, None, :]]
    b = jnp.where(valid[None, None, :, None, :], b.astype(F32), NEG)
    return b.reshape(rpb.shape[0], r_, w_, r_ * w_)


def _na_kernel(q_ref, k_ref, v_ref, b_ref, qg_ref, kg_ref, o_ref, kn_ref, *, rows, cfg):
    w_, kr = cfg.grid_w, cfg.na_rows
    band = kr * w_
    eps = cfg.eps
    kg = kg_ref[...]
    qg = qg_ref[...] * (cfg.na_dim ** -0.5)

    def knorm(i, carry):
        sl = pl.ds(pl.multiple_of(i * band, band), band)
        kk = k_ref[sl, :].astype(F32)
        ms = jnp.mean(kk * kk, axis=-1, keepdims=True)
        kn_ref[sl, :] = (kk * lax.rsqrt(ms + eps) * kg).astype(BF16)
        return carry

    lax.fori_loop(0, rows // kr, knorm, 0)

    def row(r, carry):
        r0 = jnp.clip(r - kr // 2, 0, rows - kr)
        delta = r - r0
        qsl = pl.ds(pl.multiple_of(r * w_, w_), w_)
        bsl = pl.ds(pl.multiple_of(r0 * w_, w_), band)
        q = q_ref[qsl, :].astype(F32)
        ms = jnp.mean(q * q, axis=-1, keepdims=True)
        qn = (q * lax.rsqrt(ms + eps) * qg).astype(BF16)
        s = lax.dot_general(qn, kn_ref[bsl, :], (((1,), (1,)), ((), ())), preferred_element_type=F32)
        s = s + b_ref[0, delta]
        e = jnp.exp(s - jnp.max(s, axis=-1, keepdims=True))
        den = jnp.sum(e, axis=-1, keepdims=True)
        o = jnp.dot(e.astype(BF16), v_ref[bsl, :], preferred_element_type=F32)
        o_ref[qsl, :] = (o / den).astype(o_ref.dtype)
        return carry

    lax.fori_loop(0, rows, row, 0)


def neighborhood_attention(z, bias, q_gain, k_gain, off, s_len, cfg):
    hd = cfg.na_dim
    rows = s_len // cfg.grid_w
    assert rows % cfg.na_rows == 0 and off % s_len == 0 and hd == LANES
    rb = off // s_len
    kq, kk, kv = 0, cfg.z_na_k // hd, cfg.z_na_v // hd
    band = cfg.na_rows * cfg.grid_w
    return pl.pallas_call(
        functools.partial(_na_kernel, rows=rows, cfg=cfg),
        grid=(cfg.na_heads,),
        in_specs=[
            pl.BlockSpec((s_len, hd), lambda h: (rb, kq + h)),
            pl.BlockSpec((s_len, hd), lambda h: (rb, kk + h)),
            pl.BlockSpec((s_len, hd), lambda h: (rb, kv + h)),
            pl.BlockSpec((1, cfg.na_rows, cfg.grid_w, band), lambda h: (h, 0, 0, 0)),
            pl.BlockSpec((1, hd), lambda h: (0, 0)),
            pl.BlockSpec((1, hd), lambda h: (0, 0)),
        ],
        out_specs=pl.BlockSpec((s_len, hd), lambda h: (0, h)),
        out_shape=jax.ShapeDtypeStruct((s_len, cfg.naw), BF16),
        scratch_shapes=[pltpu.VMEM((s_len, hd), BF16)],
        compiler_params=_params("parallel"),
        name="neighborhood_attention",
    )(z, z, z, bias, q_gain, k_gain)


HALO = 16


def _conv_kernel(x_ref, p_ref, n_ref, w_ref, b_ref, s_ref, o_ref, *, ts, taps, starts, ends):
    i = pl.program_id(1)
    is_start = functools.reduce(jnp.logical_or, [i == s // ts for s in starts])
    is_end = functools.reduce(jnp.logical_or, [i == e // ts - 1 for e in ends])
    xp = p_ref[...].astype(F32) * jnp.where(is_start, 0.0, 1.0)
    xn = n_ref[...].astype(F32) * jnp.where(is_end, 0.0, 1.0)
    ext = jnp.concatenate([xp, x_ref[...].astype(F32), xn], axis=0)
    w = w_ref[...]
    half = taps // 2
    acc = jnp.zeros((ts, x_ref.shape[1]), F32) + b_ref[...]
    for j in range(taps):
        lo = HALO + j - half
        acc = acc + w[j:j + 1, :] * ext[lo:lo + ts, :]
    o_ref[...] = (acc * _sigmoid(acc) * s_ref[...]).astype(o_ref.dtype)


def conv_silu(z, conv_w, conv_b, col_scale, seqs, cfg):
    n = z.shape[0]
    c = 2 * cfg.qkw
    cb = _tile(c, 512, LANES)
    ts = _tile(min(s for _, s in seqs), 1024, HALO)
    assert all(o % ts == 0 and s % ts == 0 for o, s in seqs) and cfg.z_ml_q % cb == 0
    c0 = cfg.z_ml_q // cb
    nh = n // HALO
    starts = [o for o, _ in seqs]
    ends = [o + s for o, s in seqs]
    return pl.pallas_call(
        functools.partial(_conv_kernel, ts=ts, taps=cfg.ml_conv, starts=starts, ends=ends),
        grid=(c // cb, n // ts),
        in_specs=[
            pl.BlockSpec((ts, cb), lambda j, i: (i, c0 + j)),
            pl.BlockSpec((HALO, cb), lambda j, i: (jnp.maximum(i * (ts // HALO) - 1, 0), c0 + j)),
            pl.BlockSpec((HALO, cb), lambda j, i: (jnp.minimum((i + 1) * (ts // HALO), nh - 1), c0 + j)),
            pl.BlockSpec((cfg.ml_conv, cb), lambda j, i: (0, j)),
            pl.BlockSpec((1, cb), lambda j, i: (0, j)),
            pl.BlockSpec((1, cb), lambda j, i: (0, j)),
        ],
        out_specs=pl.BlockSpec((ts, cb), lambda j, i: (i, j)),
        out_shape=jax.ShapeDtypeStruct((n, c), BF16),
        compiler_params=_params("parallel", "parallel"),
        name="conv_silu",
    )(z, z, z, conv_w, conv_b, col_scale)


def _split3(x):
    hi = x.astype(BF16)
    r = x - hi.astype(F32)
    mid = r.astype(BF16)
    lo = (r - mid.astype(F32)).astype(BF16)
    return hi, mid, lo


def _gate_kernel(g_ref, o_ref, *, nh):
    d = pl.program_id(0)
    g = g_ref[...]
    ell = g.shape[0]
    lf = jnp.minimum(g, 0.0) - jnp.log(1.0 + jnp.exp(-jnp.abs(g)))
    t = lax.broadcasted_iota(I32, (ell, ell), 0)
    u = lax.broadcasted_iota(I32, (ell, ell), 1)
    tri = jnp.where((u - t) * jnp.where(d == 0, 1, -1) <= 0, 1.0, 0.0).astype(BF16)
    cs = sum(jnp.dot(tri, p, preferred_element_type=F32) for p in _split3(lf))
    lane = lax.broadcasted_iota(I32, g.shape, 1)
    o_ref[...] = jnp.where((lane >= nh) & (lane < 2 * nh), cs, g)


def gate_cumsum(g_dir, cfg):
    _, n, _ = g_dir.shape
    ell = cfg.ml_chunk
    return pl.pallas_call(
        functools.partial(_gate_kernel, nh=cfg.ml_heads),
        grid=(2, n // ell),
        in_specs=[pl.BlockSpec((None, ell, LANES), lambda d, c: (d, c, 0))],
        out_specs=pl.BlockSpec((None, ell, LANES), lambda d, c: (d, c, 0)),
        out_shape=jax.ShapeDtypeStruct(g_dir.shape, F32),
        compiler_params=_params("parallel", "parallel"),
        name="gate_cumsum",
    )(g_dir)


def _mlstm_kernel(q_ref, kt_ref, v_ref, pc_ref, pr_ref, o_ref, c_ref, m_ref, *, nc, firsts, lasts, cfg):
    d = pl.program_id(0)
    c = pl.program_id(1)
    ci = jnp.where(d == 0, c, nc - 1 - c)
    nh, dk, dv, ell = cfg.ml_heads, cfg.ml_qk, cfg.ml_v, cfg.ml_chunk
    dve = dv + LANES

    fwd_reset = functools.reduce(jnp.logical_or, [ci == f for f in firsts])
    bwd_reset = functools.reduce(jnp.logical_or, [ci == f for f in lasts])

    @pl.when(jnp.where(d == 0, fwd_reset, bwd_reset))
    def _():
        c_ref[...] = jnp.zeros_like(c_ref)
        m_ref[...] = jnp.zeros_like(m_ref)

    t = lax.broadcasted_iota(I32, (ell, ell), 0)
    s = lax.broadcasted_iota(I32, (ell, ell), 1)
    tri = (s - t) * jnp.where(d == 0, 1, -1) <= 0
    ones_col = jnp.where(lax.broadcasted_iota(I32, (ell, LANES), 1) == 0, 1.0, 0.0).astype(BF16)
    pc = pc_ref[...]
    pr = pr_ref[...]

    for h in range(nh):
        qh = q_ref[:, h * dk:(h + 1) * dk]
        kth = kt_ref[h * dk:(h + 1) * dk, :]
        vext = jnp.concatenate([v_ref[:, h * dv:(h + 1) * dv], ones_col], axis=1)
        bcol = pc[:, nh + h:nh + h + 1]
        irow = pr[h:h + 1, :]
        brow = pr[nh + h:nh + h + 1, :]
        tot = jnp.where(d == 0, brow[:, ell - 1:ell], brow[:, 0:1])
        m_prev = m_ref[h][0:1, 0:1]
        cext = c_ref[h]

        dm = jnp.where(tri, bcol - (brow - irow), NEG)
        inter = bcol + m_prev
        m_t = jnp.maximum(inter, jnp.max(dm, axis=1, keepdims=True))
        sc = jnp.exp(inter - m_t)
        a = jnp.dot(qh, kth, preferred_element_type=F32) * jnp.exp(dm - m_t)
        nd = jnp.dot(a.astype(BF16), vext, preferred_element_type=F32)
        nd = nd + sc * jnp.dot(qh, cext.astype(BF16), preferred_element_type=F32)
        den = jnp.maximum(jnp.abs(nd[:, dv:dv + 1]), jnp.exp(-m_t))
        o_ref[:, h * dv:(h + 1) * dv] = nd[:, :dv] / den

        dend = tot - brow + irow
        m_new = jnp.maximum(tot + m_prev, jnp.max(dend, axis=1, keepdims=True))
        wk = jnp.exp(dend - m_new)
        decay = jnp.exp(tot + m_prev - m_new)
        kw = (kth.astype(F32) * wk).astype(BF16)
        c_ref[h] = decay * cext + jnp.dot(kw, vext, preferred_element_type=F32)
        m_ref[h] = jnp.broadcast_to(m_new, m_ref.shape[1:])


def mlstm(mqk, kt, z, pc, pr, seqs, cfg):
    n = z.shape[0]
    ell = cfg.ml_chunk
    nc = n // ell
    assert all(o % ell == 0 and s % ell == 0 for o, s in seqs)
    firsts = [o // ell for o, _ in seqs]
    lasts = [(o + s) // ell - 1 for o, s in seqs]
    qkw, vw = cfg.qkw, cfg.vw
    assert cfg.z_ml_v % vw == 0
    v0 = cfg.z_ml_v // vw

    def cidx(d, c):
        return jnp.where(d == 0, c, nc - 1 - c)

    return pl.pallas_call(
        functools.partial(_mlstm_kernel, nc=nc, firsts=firsts, lasts=lasts, cfg=cfg),
        grid=(2, nc),
        in_specs=[
            pl.BlockSpec((ell, qkw), lambda d, c: (cidx(d, c), 0)),
            pl.BlockSpec((qkw, ell), lambda d, c: (0, cidx(d, c))),
            pl.BlockSpec((ell, vw), lambda d, c: (cidx(d, c), v0)),
            pl.BlockSpec((None, ell, LANES), lambda d, c: (d, cidx(d, c), 0)),
            pl.BlockSpec((None, 2 * cfg.ml_heads, ell), lambda d, c: (d, 0, cidx(d, c))),
        ],
        out_specs=pl.BlockSpec((None, ell, vw), lambda d, c: (d, cidx(d, c), 0)),
        out_shape=jax.ShapeDtypeStruct((2, n, vw), F32),
        scratch_shapes=[
            pltpu.VMEM((cfg.ml_heads, cfg.ml_qk, cfg.ml_v + LANES), F32),
            pltpu.VMEM((cfg.ml_heads, 8, LANES), F32),
        ],
        compiler_params=_params("arbitrary", "arbitrary"),
        name="mlstm",
    )(mqk, kt, z, pc, pr)


def _ml_out_kernel(hf_ref, hb_ref, og_ref, g_ref, o_ref, *, nh, dv, eps):
    for h in range(nh):
        sl = slice(h * dv, (h + 1) * dv)
        x = hf_ref[:, sl] + hb_ref[:, sl]
        ms = jnp.mean(x * x, axis=-1, keepdims=True)
        y = x * lax.rsqrt(ms + eps) * g_ref[:, sl]
        o_ref[:, sl] = (y * _sigmoid(og_ref[:, sl].astype(F32))).astype(o_ref.dtype)


def ml_out(hdir, z, ml_gain, cfg):
    _, n, vw = hdir.shape
    tm = _tile(n, 512)
    assert cfg.z_ml_o % vw == 0
    o0 = cfg.z_ml_o // vw
    return pl.pallas_call(
        functools.partial(_ml_out_kernel, nh=cfg.ml_heads, dv=cfg.ml_v, eps=cfg.eps),
        grid=(n // tm,),
        in_specs=[
            pl.BlockSpec((None, tm, vw), lambda i: (0, i, 0)),
            pl.BlockSpec((None, tm, vw), lambda i: (1, i, 0)),
            pl.BlockSpec((tm, vw), lambda i: (i, o0)),
            pl.BlockSpec((1, vw), lambda i: (0, 0)),
        ],
        out_specs=pl.BlockSpec((tm, vw), lambda i: (i, 0)),
        out_shape=jax.ShapeDtypeStruct((n, vw), BF16),
        compiler_params=_params("parallel"),
        name="ml_out",
    )(hdir, hdir, z, ml_gain)


def _prefix_counts(mask_f, upper, strict_lower):
    m16 = mask_f.astype(BF16)
    local = jnp.dot(m16, upper, preferred_element_type=F32)
    rowtot = jnp.broadcast_to(local[:, LANES - 1:LANES], local.shape)
    offs = jnp.dot(strict_lower, rowtot.astype(BF16), preferred_element_type=F32)
    return local + offs, offs


def _route_kernel(a_ref, idx_ref, gate_ref, pos_ref, lo_ref, *, cap):
    a = a_ref[0]
    nb = a.shape[0]
    bits = lax.bitcast_convert_type(a, I32)

    thr = jnp.zeros((1, 1), I32)
    for bit in range(30, -1, -1):
        cand = thr | (1 << bit)
        cnt = jnp.sum(jnp.where(bits >= cand, 1.0, 0.0), keepdims=True)
        thr = jnp.where(cnt >= cap, cand, thr)

    li = lax.broadcasted_iota(I32, (LANES, LANES), 0)
    lj = lax.broadcasted_iota(I32, (LANES, LANES), 1)
    upper = jnp.where(li <= lj, 1.0, 0.0).astype(BF16)
    bi = lax.broadcasted_iota(I32, (nb, nb), 0)
    bj = lax.broadcasted_iota(I32, (nb, nb), 1)
    strict_lower = jnp.where(bj < bi, 1.0, 0.0).astype(BF16)

    gt = bits > thr
    eq = bits == thr
    need = cap - jnp.sum(jnp.where(gt, 1.0, 0.0), keepdims=True)
    eq_f = jnp.where(eq, 1.0, 0.0)
    eq_incl, _ = _prefix_counts(eq_f, upper, strict_lower)
    sel = gt | (eq & (eq_incl <= need))
    sel_f = jnp.where(sel, 1.0, 0.0)
    incl, offs = _prefix_counts(sel_f, upper, strict_lower)

    pos_ref[0] = jnp.where(sel, incl - 1.0, -1.0).astype(I32)
    lo_ref[0] = offs.astype(I32)

    p_col = lax.broadcasted_iota(I32, (cap, 1), 0).astype(F32)
    ones8 = jnp.ones((8, LANES), BF16)
    cnt_row = lax.dot_general(ones8, sel_f.astype(BF16), (((1,), (1,)), ((), ())),
                              preferred_element_type=F32)[0:1, :]
    ui = lax.broadcasted_iota(I32, (nb, nb), 0)
    uj = lax.broadcasted_iota(I32, (nb, nb), 1)
    upper_nb = jnp.where(ui <= uj, 1.0, 0.0).astype(BF16)
    cum8 = jnp.dot(jnp.broadcast_to(cnt_row, (8, nb)).astype(BF16), upper_nb, preferred_element_type=F32)
    blockcum = cum8[0:1, :]
    before = blockcum <= p_col
    jblk = jnp.sum(jnp.where(before, 1.0, 0.0), axis=1, keepdims=True)
    base = jnp.sum(jnp.where(before, cnt_row, 0.0), axis=1, keepdims=True)
    onehot = (lax.broadcasted_iota(I32, (cap, nb), 1).astype(F32) == jblk).astype(BF16)
    local_incl = incl - offs
    a_hi, a_mid, a_lo = _split3(a)
    table = jnp.concatenate([local_incl.astype(BF16), a_hi, a_mid, a_lo], axis=1)
    rows = jnp.dot(onehot, table, preferred_element_type=F32)
    rank = p_col - base + 1.0
    lane_cnt = jnp.sum(jnp.where(rows[:, :LANES] < rank, 1.0, 0.0), axis=1, keepdims=True)
    idx_ref[0] = (jblk * LANES + lane_cnt).astype(I32)
    arow = rows[:, LANES:2 * LANES] + rows[:, 2 * LANES:3 * LANES] + rows[:, 3 * LANES:]
    lane = lax.broadcasted_iota(I32, (cap, LANES), 1).astype(F32)
    gate_ref[0] = jnp.sum(jnp.where(lane == lane_cnt, arow, 0.0), axis=1, keepdims=True)


def route(aff_t, cap):
    e, s = aff_t.shape
    nb = s // LANES
    a3 = aff_t.reshape(e, nb, LANES)
    idx, gate, pos, lo = pl.pallas_call(
        functools.partial(_route_kernel, cap=cap),
        grid=(e,),
        in_specs=[pl.BlockSpec((1, nb, LANES), lambda i: (i, 0, 0))],
        out_specs=[
            pl.BlockSpec((1, cap, 1), lambda i: (i, 0, 0)),
            pl.BlockSpec((1, cap, 1), lambda i: (i, 0, 0)),
            pl.BlockSpec((1, nb, LANES), lambda i: (i, 0, 0)),
            pl.BlockSpec((1, nb, LANES), lambda i: (i, 0, 0)),
        ],
        out_shape=[
            jax.ShapeDtypeStruct((e, cap, 1), I32),
            jax.ShapeDtypeStruct((e, cap, 1), F32),
            jax.ShapeDtypeStruct((e, nb, LANES), I32),
            jax.ShapeDtypeStruct((e, nb, LANES), I32),
        ],
        compiler_params=_params("parallel"),
        name="route",
    )(a3)
    return idx[:, :, 0], gate[:, :, 0], pos.reshape(e, s), lo[:, :, 0]


def _pack_rows(x):
    n, d = x.shape
    return lax.bitcast_convert_type(x.reshape(n, d // 2, 2), jnp.uint32).reshape(n, 1, d // 2)


def _unpack_rows(x3):
    n, _, c = x3.shape
    return lax.bitcast_convert_type(x3.reshape(n, c), BF16).reshape(n, 2 * c)


def _gather_kernel(idx_ref, src_ref, dst_ref, sem, *, rows):
    base = pl.program_id(0) * rows

    def copy(r):
        return pltpu.make_async_copy(src_ref.at[idx_ref[base + r]], dst_ref.at[base + r], sem)

    def issue(r, carry):
        copy(r).start()
        return carry

    def drain(r, carry):
        copy(r).wait()
        return carry

    lax.fori_loop(0, rows, issue, 0)
    lax.fori_loop(0, rows, drain, 0)


def gather_rows(src3, gidx):
    p = gidx.shape[0]
    rows = _tile(p, 1024)
    return pl.pallas_call(
        functools.partial(_gather_kernel, rows=rows),
        grid_spec=pltpu.PrefetchScalarGridSpec(
            num_scalar_prefetch=1,
            grid=(p // rows,),
            in_specs=[pl.BlockSpec(memory_space=pl.ANY)],
            out_specs=pl.BlockSpec(memory_space=pl.ANY),
            scratch_shapes=[pltpu.SemaphoreType.DMA(())],
        ),
        out_shape=jax.ShapeDtypeStruct((p,) + src3.shape[1:], src3.dtype),
        compiler_params=pltpu.CompilerParams(dimension_semantics=("arbitrary",), has_side_effects=True),
        name="gather_rows",
    )(gidx, src3)


def _ffn_up_kernel(x_ref, wg_ref, wu_ref, o_ref):
    x = x_ref[...]
    a = jnp.dot(x, wg_ref[...], preferred_element_type=F32)
    b = jnp.dot(x, wu_ref[...], preferred_element_type=F32)
    o_ref[...] = (a * _sigmoid(a) * b).astype(o_ref.dtype)


def ffn_up(xe, w_gate, w_up, layer, n_exp):
    p, d = xe.shape
    f = w_gate.shape[-1]
    capt = p // n_exp
    tm, tn = _tile(capt, 1024, 16), _tile(f, 512, LANES)
    nt = capt // tm
    return pl.pallas_call(
        _ffn_up_kernel,
        grid=(n_exp, nt, f // tn),
        in_specs=[
            pl.BlockSpec((tm, d), lambda e, i, j: (e * nt + i, 0)),
            pl.BlockSpec((None, None, d, tn), lambda e, i, j: (layer, e, 0, j)),
            pl.BlockSpec((None, None, d, tn), lambda e, i, j: (layer, e, 0, j)),
        ],
        out_specs=pl.BlockSpec((tm, tn), lambda e, i, j: (e * nt + i, j)),
        out_shape=jax.ShapeDtypeStruct((p, f), BF16),
        compiler_params=_params("parallel", "parallel", "parallel"),
        name="ffn_up",
    )(xe, w_gate, w_up)


def _ffn_down_kernel(h_ref, w_ref, g_ref, o_ref):
    y = jnp.dot(h_ref[...], w_ref[...], preferred_element_type=F32) * g_ref[...]
    o_ref[...] = y.astype(o_ref.dtype)


def ffn_down(hid, w_down, gate_col, layer, n_exp):
    p, f = hid.shape
    d = w_down.shape[-1]
    capt = p // n_exp
    tm, tn = _tile(capt, 1024, 16), _tile(d, 1024, LANES)
    nt = capt // tm
    return pl.pallas_call(
        _ffn_down_kernel,
        grid=(n_exp, nt, d // tn),
        in_specs=[
            pl.BlockSpec((tm, f), lambda e, i, j: (e * nt + i, 0)),
            pl.BlockSpec((None, None, f, tn), lambda e, i, j: (layer, e, 0, j)),
            pl.BlockSpec((tm, 1), lambda e, i, j: (e * nt + i, 0)),
        ],
        out_specs=pl.BlockSpec((tm, tn), lambda e, i, j: (e * nt + i, j)),
        out_shape=jax.ShapeDtypeStruct((p, d), BF16),
        compiler_params=_params("parallel", "parallel", "parallel"),
        name="ffn_down",
    )(hid, w_down, gate_col)


SLAB = 64


def _combine_kernel(lo_ref, hi_ref, base_ref, h_ref, pos_ref, y_ref, o_ref, buf, sem, *, n_exp, bpt, p_rows):
    i = pl.program_id(0)
    nblk = lo_ref.shape[0] // n_exp
    tt = h_ref.shape[0]
    o_ref[...] = h_ref[...]
    jlane = lax.broadcasted_iota(I32, (tt, SLAB), 1)
    for e in range(n_exp):
        lo = lo_ref[e * nblk + i * bpt]
        hi = hi_ref[e * nblk + i * bpt + bpt - 1]
        base = base_ref[e * nblk + i * bpt]
        lo_al = (lo // 16) * 16
        nchunk = (hi - lo_al + SLAB - 1) // SLAB
        pos = pos_ref[:, e:e + 1]

        def chunk(c, carry):
            start = lo_al + c * SLAB
            row = jnp.minimum(base + start, p_rows - SLAB)
            cp = pltpu.make_async_copy(y_ref.at[pl.ds(pl.multiple_of(row, 16), SLAB)], buf, sem)
            cp.start()
            cp.wait()
            sel = ((pos - (row - base) == jlane) & (pos >= start)).astype(BF16)
            o_ref[...] += jnp.dot(sel, buf[...], preferred_element_type=F32)
            return carry

        lax.fori_loop(0, nchunk, chunk, 0)


def combine(h, pos_t, y, lo_flat, hi_flat, base_flat, n_exp):
    n, d = h.shape
    tt = _tile(n, 256, LANES)
    return pl.pallas_call(
        functools.partial(_combine_kernel, n_exp=n_exp, bpt=tt // LANES, p_rows=y.shape[0]),
        grid_spec=pltpu.PrefetchScalarGridSpec(
            num_scalar_prefetch=3,
            grid=(n // tt,),
            in_specs=[
                pl.BlockSpec((tt, d), lambda i, *_: (i, 0)),
                pl.BlockSpec((tt, LANES), lambda i, *_: (i, 0)),
                pl.BlockSpec(memory_space=pl.ANY),
            ],
            out_specs=pl.BlockSpec((tt, d), lambda i, *_: (i, 0)),
            scratch_shapes=[pltpu.VMEM((SLAB, d), BF16), pltpu.SemaphoreType.DMA(())],
        ),
        out_shape=jax.ShapeDtypeStruct((n, d), F32),
        compiler_params=_params("arbitrary"),
        name="combine",
    )(lo_flat, hi_flat, base_flat, h, pos_t, y)


def expert_choice_ffn(h, norm_g, w_router_pad, w_gate, w_up, w_down, layer, seqs, cfg):
    n, d = h.shape
    ne = cfg.n_experts
    hn, aff = norm_proj(h, norm_g, w_router_pad, jnp.zeros((1, LANES), F32), eps=cfg.eps, softmax_cols=ne)
    aff_t = aff[:, :ne].T
    caps = [cfg.ec_capacity * s // ne for _, s in seqs]
    capt = sum(caps)
    idx_l, gate_l, pos_l, lo_l, hi_l, base_l = [], [], [], [], [], []
    row0 = 0
    for (off, s), cap in zip(seqs, caps):
        idx, gate, pos, lo = route(aff_t[:, off:off + s], cap)
        idx_l.append(idx + off)
        gate_l.append(gate)
        pos_l.append(pos)
        lo_l.append(lo)
        hi_l.append(jnp.concatenate([lo[:, 1:], jnp.full((ne, 1), cap, I32)], axis=1))
        base_l.append(jnp.broadcast_to((jnp.arange(ne, dtype=I32) * capt + row0)[:, None], lo.shape))
        row0 += cap
    gidx = jnp.concatenate(idx_l, axis=1).reshape(-1)
    gate_col = jnp.concatenate(gate_l, axis=1).reshape(-1, 1)
    pos_t = _pad_cols(jnp.concatenate(pos_l, axis=1).T)
    lo_flat, hi_flat, base_flat = (jnp.concatenate(t, axis=1).reshape(-1) for t in (lo_l, hi_l, base_l))

    xe = _unpack_rows(gather_rows(_pack_rows(hn), gidx))
    hid = ffn_up(xe, w_gate, w_up, layer, ne)
    y = ffn_down(hid, w_down, gate_col, layer, ne)
    return combine(h, pos_t, y, lo_flat, hi_flat, base_flat, ne)


def _pad_cols(w, cols=LANES):
    return jnp.pad(w, [(0, 0)] * (w.ndim - 1) + [(0, cols - w.shape[-1])])


def trunk(x, seqs, p, cfg):
    n, d = x.shape
    nh = cfg.ml_heads
    depth = p["w_in"].shape[0]
    o_g = 3 * cfg.naw + 2 * cfg.qkw + 2 * cfg.vw
    w_in = p["w_in"]
    w_main = jnp.concatenate([w_in[:, :, :o_g], w_in[:, :, o_g + 4 * nh:]], axis=-1).astype(BF16)
    w_gates = _pad_cols(w_in[:, :, o_g:o_g + 4 * nh]).astype(BF16)
    b_gates = _pad_cols(p["b_gate"])[:, None, :]
    w_pa, w_pm, w_out = (p[k].astype(BF16) for k in ("w_pa", "w_pm", "w_out"))
    w_router = _pad_cols(p["w_router"]).astype(BF16)
    w_fg, w_fu, w_fd = (p[k].astype(BF16) for k in ("w_ff_gate", "w_ff_up", "w_ff_down"))
    col_scale = jnp.concatenate([jnp.full((1, cfg.qkw), cfg.ml_qk ** -0.5, F32), jnp.ones((1, cfg.qkw), F32)], axis=1)

    for l in range(depth):
        xn, gates = norm_proj(x, p["norm1_g"][l][None], w_gates[l], b_gates[l], eps=cfg.eps)
        z = in_proj(xn, w_main, l)
        bias = na_bias_table(p["rpb"][l], cfg)
        ya = jnp.concatenate(
            [neighborhood_attention(z, bias, p["q_gain"][l][None], p["k_gain"][l][None], off, s, cfg)
             for off, s in seqs], axis=0)
        mqk = conv_silu(z, p["conv_w"][l], p["conv_b"][l][None], col_scale, seqs, cfg)
        kt = mqk[:, cfg.qkw:].T
        g_dir = jnp.stack([_pad_cols(gates[:, 0:2 * nh]), _pad_cols(gates[:, 2 * nh:4 * nh])])
        pc = gate_cumsum(g_dir, cfg)
        pr = jnp.transpose(pc[:, :, :2 * nh], (0, 2, 1))
        hdir = mlstm(mqk, kt, z, pc, pr, seqs, cfg)
        ym = ml_out(hdir, z, p["ml_gain"][l].reshape(1, -1), cfg)
        merged = merge_proj(ya, ym, w_pa, w_pm, z, l, cfg)
        h = out_proj(merged, w_out, x, l)
        x = expert_choice_ffn(h, p["norm2_g"][l][None], w_router[l], w_fg, w_fu, w_fd, l, seqs, cfg)
    return x


def kernel(x_prompt, x_sample, norm1_g, norm2_g, w_in, b_gate, conv_w, conv_b, q_gain, k_gain, rpb, ml_gain,
           w_pa, w_pm, w_out, w_router, w_ff_gate, w_ff_up, w_ff_down):
    cfg = CFG
    d = x_prompt.shape[-1]
    xs = [x_prompt.reshape(-1, d), x_sample.reshape(-1, d)]
    seqs, off = [], 0
    for xx in (x_prompt, x_sample):
        for _ in range(xx.shape[0]):
            seqs.append((off, xx.shape[1]))
            off += xx.shape[1]
    p = dict(norm1_g=norm1_g, norm2_g=norm2_g, w_in=w_in, b_gate=b_gate, conv_w=conv_w, conv_b=conv_b,
             q_gain=q_gain, k_gain=k_gain, rpb=rpb, ml_gain=ml_gain, w_pa=w_pa, w_pm=w_pm, w_out=w_out,
             w_router=w_router, w_ff_gate=w_ff_gate, w_ff_up=w_ff_up, w_ff_down=w_ff_down)
    y = trunk(jnp.concatenate(xs, axis=0), seqs, p, cfg)
    n_p = xs[0].shape[0]
    return (y[:n_p].reshape(x_prompt.shape), y[n_p:].reshape(x_sample.shape))
```

```python
import dataclasses
import functools

import jax
import jax.numpy as jnp
import numpy as np
from jax import lax
from jax.experimental import pallas as pl
from jax.experimental.pallas import tpu as pltpu

F32 = jnp.float32
BF16 = jnp.bfloat16
I32 = jnp.int32
NEG = -1e30
LANES = 128
VMEM_LIMIT = 56 * 1024 * 1024


@dataclasses.dataclass(frozen=True)
class Cfg:
    d_model: int = 4096
    grid_w: int = 64
    na_heads: int = 16
    na_dim: int = 128
    na_rows: int = 8
    na_cols: int = 16
    ml_heads: int = 8
    ml_qk: int = 128
    ml_v: int = 256
    ml_conv: int = 5
    n_experts: int = 16
    ec_capacity: int = 2
    d_ff: int = 2048
    eps: float = 1e-6
    ml_chunk: int = 256

    @property
    def naw(self):
        return self.na_heads * self.na_dim

    @property
    def qkw(self):
        return self.ml_heads * self.ml_qk

    @property
    def vw(self):
        return self.ml_heads * self.ml_v

    @property
    def z_na_k(self):
        return self.naw

    @property
    def z_na_v(self):
        return 2 * self.naw

    @property
    def z_ml_q(self):
        return 3 * self.naw

    @property
    def z_ml_v(self):
        return self.z_ml_q + 2 * self.qkw

    @property
    def z_ml_o(self):
        return self.z_ml_v + self.vw

    @property
    def z_merge(self):
        return self.z_ml_o + self.vw

    @property
    def z_cols(self):
        return self.z_merge + 2 * self.d_model


CFG = Cfg()


def _tile(dim, pref, mult=8):
    if dim <= pref:
        return dim
    t = (pref // mult) * mult
    while t > mult and dim % t:
        t -= mult
    assert dim % t == 0, (dim, pref, mult)
    return t


def _params(*sem):
    return pltpu.CompilerParams(dimension_semantics=sem, vmem_limit_bytes=VMEM_LIMIT)


def _sigmoid(x):
    return 1.0 / (1.0 + jnp.exp(-x))


def _pack_halves(x):
    half = x.shape[1] // 2
    lo = lax.bitcast_convert_type(x[:, :half].astype(F32), jnp.uint32)
    hi = lax.bitcast_convert_type(x[:, half:].astype(F32), jnp.uint32)
    return (lo >> 16) | (hi & jnp.uint32(0xFFFF0000))


def _unpack_halves(w):
    lo = lax.bitcast_convert_type(w << 16, F32).astype(BF16)
    hi = lax.bitcast_convert_type(w & jnp.uint32(0xFFFF0000), F32).astype(BF16)
    return jnp.concatenate([lo, hi], axis=1)


def _norm_proj_kernel(x_ref, g_ref, w_ref, b_ref, xn_ref, s_ref, *, eps, softmax_cols, packed):
    x = x_ref[...]
    ms = jnp.mean(x * x, axis=-1, keepdims=True)
    xn = (x * lax.rsqrt(ms + eps) * g_ref[...]).astype(BF16)
    xn_ref[...] = _pack_halves(xn) if packed else xn
    s = jnp.dot(xn, w_ref[...], preferred_element_type=F32) + b_ref[...]
    if softmax_cols:
        lane = lax.broadcasted_iota(I32, s.shape, 1)
        s = jnp.where(lane < softmax_cols, s, NEG)
        e = jnp.exp(s - jnp.max(s, axis=-1, keepdims=True))
        s = e / jnp.sum(e, axis=-1, keepdims=True)
    s_ref[...] = s


def norm_proj(x, g, w_small, b_small, *, eps, softmax_cols=0, packed=False):
    n, d = x.shape
    tm = _tile(n, 512)
    xn_t = jax.eval_shape(_pack_halves, jax.ShapeDtypeStruct((n, d), BF16)) if packed else jax.ShapeDtypeStruct((n, d), BF16)
    return pl.pallas_call(
        functools.partial(_norm_proj_kernel, eps=eps, softmax_cols=softmax_cols, packed=packed),
        grid=(n // tm,),
        in_specs=[
            pl.BlockSpec((tm, d), lambda i: (i, 0)),
            pl.BlockSpec((1, d), lambda i: (0, 0)),
            pl.BlockSpec((d, LANES), lambda i: (0, 0)),
            pl.BlockSpec((1, LANES), lambda i: (0, 0)),
        ],
        out_specs=[
            pl.BlockSpec((tm, xn_t.shape[1]), lambda i: (i, 0)),
            pl.BlockSpec((tm, LANES), lambda i: (i, 0)),
        ],
        out_shape=[xn_t, jax.ShapeDtypeStruct((n, LANES), F32)],
        compiler_params=_params("parallel"),
        name="norm_proj",
    )(x, g, w_small, b_small)


def _mm_kernel(x_ref, w_ref, o_ref):
    o_ref[...] = jnp.dot(x_ref[...], w_ref[...], preferred_element_type=F32).astype(o_ref.dtype)


def in_proj(xn, w_main, layer, tm_pref=1024, tn_pref=1024, x_buffers=2):
    n, d = xn.shape
    zc = w_main.shape[-1]
    tm, tn = _tile(n, tm_pref), _tile(zc, tn_pref, LANES)
    return pl.pallas_call(
        _mm_kernel,
        grid=(n // tm, zc // tn),
        in_specs=[
            pl.BlockSpec((tm, d), lambda i, j: (i, 0), pipeline_mode=pl.Buffered(x_buffers)),
            pl.BlockSpec((None, d, tn), lambda i, j: (layer, 0, j)),
        ],
        out_specs=pl.BlockSpec((tm, tn), lambda i, j: (i, j)),
        out_shape=jax.ShapeDtypeStruct((n, zc), BF16),
        compiler_params=_params("parallel", "parallel"),
        name="in_proj",
    )(xn, w_main)


def _merge_kernel(ya_ref, ym_ref, wa_ref, wm_ref, ga_ref, gm_ref, o_ref):
    a = jnp.dot(ya_ref[...], wa_ref[...], preferred_element_type=F32)
    m = jnp.dot(ym_ref[...], wm_ref[...], preferred_element_type=F32)
    o = _sigmoid(ga_ref[...].astype(F32)) * a + _sigmoid(gm_ref[...].astype(F32)) * m
    o_ref[...] = o.astype(o_ref.dtype)


def merge_proj(ya, ym, w_pa, w_pm, z, layer, cfg):
    n = ya.shape[0]
    d = cfg.d_model
    tm, tn = _tile(n, 512), _tile(d, 1024, LANES)
    assert cfg.z_merge % tn == 0
    ga0 = cfg.z_merge // tn
    gm0 = (cfg.z_merge + d) // tn
    return pl.pallas_call(
        _merge_kernel,
        grid=(n // tm, d // tn),
        in_specs=[
            pl.BlockSpec((tm, cfg.naw), lambda i, j: (i, 0)),
            pl.BlockSpec((tm, cfg.vw), lambda i, j: (i, 0)),
            pl.BlockSpec((None, cfg.naw, tn), lambda i, j: (layer, 0, j)),
            pl.BlockSpec((None, cfg.vw, tn), lambda i, j: (layer, 0, j)),
            pl.BlockSpec((tm, tn), lambda i, j: (i, ga0 + j)),
            pl.BlockSpec((tm, tn), lambda i, j: (i, gm0 + j)),
        ],
        out_specs=pl.BlockSpec((tm, tn), lambda i, j: (i, j)),
        out_shape=jax.ShapeDtypeStruct((n, d), BF16),
        compiler_params=_params("parallel", "parallel"),
        name="merge_proj",
    )(ya, ym, w_pa, w_pm, z, z)


def _out_proj_kernel(m_ref, w_ref, r_ref, o_ref):
    o_ref[...] = r_ref[...] + jnp.dot(m_ref[...], w_ref[...], preferred_element_type=F32)


def out_proj(merged, w_out, res, layer):
    n, d = merged.shape
    tm, tn = _tile(n, 512), _tile(d, 1024, LANES)
    return pl.pallas_call(
        _out_proj_kernel,
        grid=(n // tm, d // tn),
        in_specs=[
            pl.BlockSpec((tm, d), lambda i, j: (i, 0)),
            pl.BlockSpec((None, d, tn), lambda i, j: (layer, 0, j)),
            pl.BlockSpec((tm, tn), lambda i, j: (i, j)),
        ],
        out_specs=pl.BlockSpec((tm, tn), lambda i, j: (i, j)),
        out_shape=jax.ShapeDtypeStruct((n, d), F32),
        compiler_params=_params("parallel", "parallel"),
        name="out_proj",
    )(merged, w_out, res)


NA_GROUP = 4
NA_CHUNK_ROWS = 64
NA_UNROLL = 16


def na_bias_table(rpb, cfg):
    w_, kr, nc, gr = cfg.grid_w, cfg.na_rows, cfg.na_cols, NA_GROUP
    ur = kr + gr
    assert gr <= kr // 2
    cols = np.arange(w_)
    col0 = np.clip(cols - nc // 2, 0, w_ - nc)
    valid_c = (cols[None, :] >= col0[:, None]) & (cols[None, :] < col0[:, None] + nc)
    dc = cols[None, :] - cols[:, None] + (nc - 1)
    cm = (np.arange(2 * nc - 1)[:, None, None] == dc[None]) & valid_c[None]
    j = np.arange(gr)
    off = np.stack([0 * j, j, 0 * j + (ur - kr)])
    delta = np.stack([j, 0 * j + kr // 2, kr - gr + j])
    k = np.arange(ur)[None, None, :] - off[:, :, None]
    valid_r = (k >= 0) & (k < kr)
    dr = k - delta[:, :, None] + (kr - 1)
    rm = (np.arange(2 * kr - 1)[:, None, None, None] == dr[None]) & valid_r[None]
    t = jnp.einsum("lhab,atju,bwc->lhtjwuc", rpb.astype(F32), rm.astype(np.float32), cm.astype(np.float32),
                   precision=lax.Precision.HIGHEST)
    valid = valid_r[:, :, None, :, None] & valid_c[None, None, :, None, :]
    t = jnp.where(valid[None, None], t, NEG)
    return t.reshape(rpb.shape[0], rpb.shape[1], 3, gr * w_, ur * w_)


def _na_kernel(q_ref, k_ref, v_ref, b_ref, qg_ref, kg_ref, o_ref, kn_ref, vx_ref, *, row_bounds, chunk_rows, cfg):
    w_, kr, gr = cfg.grid_w, cfg.na_rows, NA_GROUP
    band = kr * w_
    gq = gr * w_
    un = (kr + gr) * w_
    eps = cfg.eps
    c = pl.program_id(1)
    n_rows = k_ref.shape[0] // w_

    @pl.when(c == 0)
    def _():
        kg = kg_ref[...]

        ones_col = jnp.where(lax.broadcasted_iota(I32, (band, LANES), 1) == 0, 1.0, 0.0).astype(BF16)

        def knorm(i, carry):
            sl = pl.ds(pl.multiple_of(i * band, band), band)
            kk = k_ref[sl, :].astype(F32)
            ms = jnp.mean(kk * kk, axis=-1, keepdims=True)
            kn_ref[sl, :] = (kk * lax.rsqrt(ms + eps) * kg).astype(BF16)
            vx_ref[sl, :] = jnp.concatenate([v_ref[sl, :], ones_col], axis=1)
            return carry

        lax.fori_loop(0, n_rows // kr, knorm, 0)

    qg = qg_ref[...] * (cfg.na_dim ** -0.5)
    g0 = c * chunk_rows
    lo = functools.reduce(lambda acc, b: jnp.where(g0 >= b[0], b[0], acc), row_bounds, 0)
    hi = functools.reduce(lambda acc, b: jnp.where(g0 >= b[0], b[1], acc), row_bounds, 0)

    def group(i, carry):
        first = g0 + i * gr
        u0 = jnp.clip(first - kr // 2, lo, hi - (kr + gr))
        kind = jnp.where(first == lo, 0, jnp.where(first == hi - gr, 2, 1))
        qsl = pl.ds(pl.multiple_of(i * gq, gq), gq)
        usl = pl.ds(pl.multiple_of(u0 * w_, w_), un)
        q = q_ref[qsl, :].astype(F32)
        ms = jnp.mean(q * q, axis=-1, keepdims=True)
        qn = (q * lax.rsqrt(ms + eps) * qg).astype(BF16)
        s = lax.dot_general(qn, kn_ref[usl, :], (((1,), (1,)), ((), ())), preferred_element_type=F32)
        s = s + b_ref[0, kind]
        e = jnp.exp((s - jnp.max(s, axis=-1, keepdims=True)).astype(BF16))
        o = jnp.dot(e, vx_ref[usl, :], preferred_element_type=F32)
        o_ref[qsl, :] = (o[:, :LANES] / o[:, LANES:LANES + 1]).astype(o_ref.dtype)
        return carry

    lax.fori_loop(0, chunk_rows // gr, group, 0, unroll=min(NA_UNROLL, chunk_rows // gr))


def neighborhood_attention(z, bias, q_gain, k_gain, seqs, cfg):
    n = z.shape[0]
    hd = cfg.na_dim
    w_ = cfg.grid_w
    chunk_rows = functools.reduce(np.gcd, [s // w_ for _, s in seqs] + [o // w_ for o, _ in seqs] + [NA_CHUNK_ROWS])
    cq = int(chunk_rows) * w_
    gr, ur = NA_GROUP, cfg.na_rows + NA_GROUP
    assert hd == LANES and all(o % cq == 0 and s % cq == 0 and s // w_ >= ur for o, s in seqs)
    assert chunk_rows % gr == 0 and (n // w_) % cfg.na_rows == 0
    row_bounds = [(o // w_, (o + s) // w_) for o, s in seqs]
    kq, kk, kv = 0, cfg.z_na_k // hd, cfg.z_na_v // hd
    return pl.pallas_call(
        functools.partial(_na_kernel, row_bounds=row_bounds, chunk_rows=int(chunk_rows), cfg=cfg),
        grid=(cfg.na_heads, n // cq),
        in_specs=[
            pl.BlockSpec((cq, hd), lambda h, c: (c, kq + h)),
            pl.BlockSpec((n, hd), lambda h, c: (0, kk + h), pipeline_mode=pl.Buffered(1)),
            pl.BlockSpec((n, hd), lambda h, c: (0, kv + h), pipeline_mode=pl.Buffered(1)),
            pl.BlockSpec((1, 3, gr * w_, ur * w_), lambda h, c: (h, 0, 0, 0)),
            pl.BlockSpec((1, hd), lambda h, c: (0, 0)),
            pl.BlockSpec((1, hd), lambda h, c: (0, 0)),
        ],
        out_specs=pl.BlockSpec((cq, hd), lambda h, c: (c, h)),
        out_shape=jax.ShapeDtypeStruct((n, cfg.naw), BF16),
        scratch_shapes=[pltpu.VMEM((n, hd), BF16), pltpu.VMEM((n, 2 * hd), BF16)],
        compiler_params=_params("arbitrary", "arbitrary"),
        name="neighborhood_attention",
    )(z, z, z, bias, q_gain, k_gain)


HALO = 16


def _conv_kernel(x_ref, p_ref, n_ref, w_ref, b_ref, s_ref, o_ref, *, ts, taps, starts, ends):
    i = pl.program_id(1)
    is_start = functools.reduce(jnp.logical_or, [i == s // ts for s in starts])
    is_end = functools.reduce(jnp.logical_or, [i == e // ts - 1 for e in ends])
    xp = p_ref[...].astype(F32) * jnp.where(is_start, 0.0, 1.0)
    xn = n_ref[...].astype(F32) * jnp.where(is_end, 0.0, 1.0)
    ext = jnp.concatenate([xp, x_ref[...].astype(F32), xn], axis=0)
    w = w_ref[...]
    half = taps // 2
    acc = jnp.zeros((ts, x_ref.shape[1]), F32) + b_ref[...]
    for j in range(taps):
        lo = HALO + j - half
        acc = acc + w[j:j + 1, :] * ext[lo:lo + ts, :]
    o_ref[...] = (acc * _sigmoid(acc) * s_ref[...]).astype(o_ref.dtype)


def conv_silu(z, conv_w, conv_b, col_scale, seqs, cfg):
    n = z.shape[0]
    c = 2 * cfg.qkw
    cb = _tile(c, 512, LANES)
    ts = _tile(min(s for _, s in seqs), 1024, HALO)
    assert all(o % ts == 0 and s % ts == 0 for o, s in seqs) and cfg.z_ml_q % cb == 0
    c0 = cfg.z_ml_q // cb
    nh = n // HALO
    starts = [o for o, _ in seqs]
    ends = [o + s for o, s in seqs]
    return pl.pallas_call(
        functools.partial(_conv_kernel, ts=ts, taps=cfg.ml_conv, starts=starts, ends=ends),
        grid=(c // cb, n // ts),
        in_specs=[
            pl.BlockSpec((ts, cb), lambda j, i: (i, c0 + j)),
            pl.BlockSpec((HALO, cb), lambda j, i: (jnp.maximum(i * (ts // HALO) - 1, 0), c0 + j)),
            pl.BlockSpec((HALO, cb), lambda j, i: (jnp.minimum((i + 1) * (ts // HALO), nh - 1), c0 + j)),
            pl.BlockSpec((cfg.ml_conv, cb), lambda j, i: (0, j)),
            pl.BlockSpec((1, cb), lambda j, i: (0, j)),
            pl.BlockSpec((1, cb), lambda j, i: (0, j)),
        ],
        out_specs=pl.BlockSpec((ts, cb), lambda j, i: (i, j)),
        out_shape=jax.ShapeDtypeStruct((n, c), BF16),
        compiler_params=_params("parallel", "parallel"),
        name="conv_silu",
    )(z, z, z, conv_w, conv_b, col_scale)


def _split3(x):
    hi = x.astype(BF16)
    r = x - hi.astype(F32)
    mid = r.astype(BF16)
    lo = (r - mid.astype(F32)).astype(BF16)
    return hi, mid, lo


def _gate_kernel(g_ref, o_ref, *, nh):
    d = pl.program_id(0)
    g = g_ref[...]
    ell = g.shape[0]
    lf = jnp.minimum(g, 0.0) - jnp.log(1.0 + jnp.exp(-jnp.abs(g)))
    t = lax.broadcasted_iota(I32, (ell, ell), 0)
    u = lax.broadcasted_iota(I32, (ell, ell), 1)
    tri = jnp.where((u - t) * jnp.where(d == 0, 1, -1) <= 0, 1.0, 0.0).astype(BF16)
    cs = sum(jnp.dot(tri, p, preferred_element_type=F32) for p in _split3(lf))
    lane = lax.broadcasted_iota(I32, g.shape, 1)
    o_ref[...] = jnp.where((lane >= nh) & (lane < 2 * nh), cs, g)


def gate_cumsum(g_dir, cfg):
    _, n, _ = g_dir.shape
    ell = cfg.ml_chunk
    return pl.pallas_call(
        functools.partial(_gate_kernel, nh=cfg.ml_heads),
        grid=(2, n // ell),
        in_specs=[pl.BlockSpec((None, ell, LANES), lambda d, c: (d, c, 0))],
        out_specs=pl.BlockSpec((None, ell, LANES), lambda d, c: (d, c, 0)),
        out_shape=jax.ShapeDtypeStruct(g_dir.shape, F32),
        compiler_params=_params("parallel", "parallel"),
        name="gate_cumsum",
    )(g_dir)


def _mlstm_kernel(q_ref, kt_ref, v_ref, pc_ref, pr_ref, o_ref, c_ref, m_ref, *, nc, firsts, lasts, cfg):
    d = pl.program_id(0)
    c = pl.program_id(1)
    ci = jnp.where(d == 0, c, nc - 1 - c)
    nh, dk, dv, ell = cfg.ml_heads, cfg.ml_qk, cfg.ml_v, cfg.ml_chunk
    dve = dv + LANES

    fwd_reset = functools.reduce(jnp.logical_or, [ci == f for f in firsts])
    bwd_reset = functools.reduce(jnp.logical_or, [ci == f for f in lasts])

    @pl.when(jnp.where(d == 0, fwd_reset, bwd_reset))
    def _():
        c_ref[...] = jnp.zeros_like(c_ref)
        m_ref[...] = jnp.zeros_like(m_ref)

    t = lax.broadcasted_iota(I32, (ell, ell), 0)
    s = lax.broadcasted_iota(I32, (ell, ell), 1)
    tri = (s - t) * jnp.where(d == 0, 1, -1) <= 0
    ones_col = jnp.where(lax.broadcasted_iota(I32, (ell, LANES), 1) == 0, 1.0, 0.0).astype(BF16)
    pc = pc_ref[...]
    pr = pr_ref[...]

    for h in range(nh):
        qh = q_ref[:, h * dk:(h + 1) * dk]
        kth = kt_ref[h * dk:(h + 1) * dk, :]
        vext = jnp.concatenate([v_ref[:, h * dv:(h + 1) * dv], ones_col], axis=1)
        bcol = pc[:, nh + h:nh + h + 1]
        irow = pr[h:h + 1, :]
        brow = pr[nh + h:nh + h + 1, :]
        tot = jnp.where(d == 0, brow[:, ell - 1:ell], brow[:, 0:1])
        m_prev = m_ref[h][0:1, 0:1]
        cext = c_ref[h]

        dm = jnp.where(tri, bcol - (brow - irow), NEG)
        inter = bcol + m_prev
        m_t = jnp.maximum(inter, jnp.max(dm, axis=1, keepdims=True))
        sc = jnp.exp(inter - m_t)
        a = jnp.dot(qh, kth, preferred_element_type=F32) * jnp.exp(dm - m_t)
        nd = jnp.dot(a.astype(BF16), vext, preferred_element_type=F32)
        nd = nd + sc * jnp.dot(qh, cext.astype(BF16), preferred_element_type=F32)
        den = jnp.maximum(jnp.abs(nd[:, dv:dv + 1]), jnp.exp(-m_t))
        o_ref[:, h * dv:(h + 1) * dv] = nd[:, :dv] / den

        dend = tot - brow + irow
        m_new = jnp.maximum(tot + m_prev, jnp.max(dend, axis=1, keepdims=True))
        wk = jnp.exp(dend - m_new)
        decay = jnp.exp(tot + m_prev - m_new)
        kw = (kth.astype(F32) * wk).astype(BF16)
        c_ref[h] = decay * cext + jnp.dot(kw, vext, preferred_element_type=F32)
        m_ref[h] = jnp.broadcast_to(m_new, m_ref.shape[1:])


def mlstm(mqk, kt, z, pc, pr, seqs, cfg):
    n = z.shape[0]
    ell = cfg.ml_chunk
    nc = n // ell
    assert all(o % ell == 0 and s % ell == 0 for o, s in seqs)
    firsts = [o // ell for o, _ in seqs]
    lasts = [(o + s) // ell - 1 for o, s in seqs]
    qkw, vw = cfg.qkw, cfg.vw
    assert cfg.z_ml_v % vw == 0
    v0 = cfg.z_ml_v // vw

    def cidx(d, c):
        return jnp.where(d == 0, c, nc - 1 - c)

    return pl.pallas_call(
        functools.partial(_mlstm_kernel, nc=nc, firsts=firsts, lasts=lasts, cfg=cfg),
        grid=(2, nc),
        in_specs=[
            pl.BlockSpec((ell, qkw), lambda d, c: (cidx(d, c), 0)),
            pl.BlockSpec((qkw, ell), lambda d, c: (0, cidx(d, c))),
            pl.BlockSpec((ell, vw), lambda d, c: (cidx(d, c), v0)),
            pl.BlockSpec((None, ell, LANES), lambda d, c: (d, cidx(d, c), 0)),
            pl.BlockSpec((None, 2 * cfg.ml_heads, ell), lambda d, c: (d, 0, cidx(d, c))),
        ],
        out_specs=pl.BlockSpec((None, ell, vw), lambda d, c: (d, cidx(d, c), 0)),
        out_shape=jax.ShapeDtypeStruct((2, n, vw), F32),
        scratch_shapes=[
            pltpu.VMEM((cfg.ml_heads, cfg.ml_qk, cfg.ml_v + LANES), F32),
            pltpu.VMEM((cfg.ml_heads, 8, LANES), F32),
        ],
        compiler_params=_params("arbitrary", "arbitrary"),
        name="mlstm",
    )(mqk, kt, z, pc, pr)


def _ml_out_kernel(hf_ref, hb_ref, og_ref, g_ref, o_ref, *, nh, dv, eps):
    for h in range(nh):
        sl = slice(h * dv, (h + 1) * dv)
        x = hf_ref[:, sl] + hb_ref[:, sl]
        ms = jnp.mean(x * x, axis=-1, keepdims=True)
        y = x * lax.rsqrt(ms + eps) * g_ref[:, sl]
        o_ref[:, sl] = (y * _sigmoid(og_ref[:, sl].astype(F32))).astype(o_ref.dtype)


def ml_out(hdir, z, ml_gain, cfg):
    _, n, vw = hdir.shape
    tm = _tile(n, 512)
    assert cfg.z_ml_o % vw == 0
    o0 = cfg.z_ml_o // vw
    return pl.pallas_call(
        functools.partial(_ml_out_kernel, nh=cfg.ml_heads, dv=cfg.ml_v, eps=cfg.eps),
        grid=(n // tm,),
        in_specs=[
            pl.BlockSpec((None, tm, vw), lambda i: (0, i, 0)),
            pl.BlockSpec((None, tm, vw), lambda i: (1, i, 0)),
            pl.BlockSpec((tm, vw), lambda i: (i, o0)),
            pl.BlockSpec((1, vw), lambda i: (0, 0)),
        ],
        out_specs=pl.BlockSpec((tm, vw), lambda i: (i, 0)),
        out_shape=jax.ShapeDtypeStruct((n, vw), BF16),
        compiler_params=_params("parallel"),
        name="ml_out",
    )(hdir, hdir, z, ml_gain)


def _prefix_counts(mask_f, upper, strict_lower):
    m16 = mask_f.astype(BF16)
    local = jnp.dot(m16, upper, preferred_element_type=F32)
    rowtot = jnp.broadcast_to(local[:, LANES - 1:LANES], local.shape)
    offs = jnp.dot(strict_lower, rowtot.astype(BF16), preferred_element_type=F32)
    return local + offs, offs


def _route_kernel(a_ref, idx_ref, gate_ref, pos_ref, lo_ref, *, cap):
    a = a_ref[0]
    nb = a.shape[0]
    bits = lax.bitcast_convert_type(a, I32)

    thr = jnp.zeros((1, 1), I32)
    for bit in range(30, -1, -1):
        cand = thr | (1 << bit)
        cnt = jnp.sum(jnp.where(bits >= cand, 1.0, 0.0), keepdims=True)
        thr = jnp.where(cnt >= cap, cand, thr)

    li = lax.broadcasted_iota(I32, (LANES, LANES), 0)
    lj = lax.broadcasted_iota(I32, (LANES, LANES), 1)
    upper = jnp.where(li <= lj, 1.0, 0.0).astype(BF16)
    bi = lax.broadcasted_iota(I32, (nb, nb), 0)
    bj = lax.broadcasted_iota(I32, (nb, nb), 1)
    strict_lower = jnp.where(bj < bi, 1.0, 0.0).astype(BF16)

    gt = bits > thr
    eq = bits == thr
    need = cap - jnp.sum(jnp.where(gt, 1.0, 0.0), keepdims=True)
    eq_f = jnp.where(eq, 1.0, 0.0)
    eq_incl, _ = _prefix_counts(eq_f, upper, strict_lower)
    sel = gt | (eq & (eq_incl <= need))
    sel_f = jnp.where(sel, 1.0, 0.0)
    incl, offs = _prefix_counts(sel_f, upper, strict_lower)

    pos_ref[0] = jnp.where(sel, incl - 1.0, -1.0).astype(I32)
    lo_ref[0] = offs.astype(I32)

    p_col = lax.broadcasted_iota(I32, (cap, 1), 0).astype(F32)
    ones8 = jnp.ones((8, LANES), BF16)
    cnt_row = lax.dot_general(ones8, sel_f.astype(BF16), (((1,), (1,)), ((), ())),
                              preferred_element_type=F32)[0:1, :]
    ui = lax.broadcasted_iota(I32, (nb, nb), 0)
    uj = lax.broadcasted_iota(I32, (nb, nb), 1)
    upper_nb = jnp.where(ui <= uj, 1.0, 0.0).astype(BF16)
    cum8 = jnp.dot(jnp.broadcast_to(cnt_row, (8, nb)).astype(BF16), upper_nb, preferred_element_type=F32)
    blockcum = cum8[0:1, :]
    before = blockcum <= p_col
    jblk = jnp.sum(jnp.where(before, 1.0, 0.0), axis=1, keepdims=True)
    base = jnp.sum(jnp.where(before, cnt_row, 0.0), axis=1, keepdims=True)
    onehot = (lax.broadcasted_iota(I32, (cap, nb), 1).astype(F32) == jblk).astype(BF16)
    local_incl = incl - offs
    a_hi, a_mid, a_lo = _split3(a)
    table = jnp.concatenate([local_incl.astype(BF16), a_hi, a_mid, a_lo], axis=1)
    rows = jnp.dot(onehot, table, preferred_element_type=F32)
    rank = p_col - base + 1.0
    lane_cnt = jnp.sum(jnp.where(rows[:, :LANES] < rank, 1.0, 0.0), axis=1, keepdims=True)
    idx_ref[0] = (jblk * LANES + lane_cnt).astype(I32)
    arow = rows[:, LANES:2 * LANES] + rows[:, 2 * LANES:3 * LANES] + rows[:, 3 * LANES:]
    lane = lax.broadcasted_iota(I32, (cap, LANES), 1).astype(F32)
    gate_ref[0] = jnp.sum(jnp.where(lane == lane_cnt, arow, 0.0), axis=1, keepdims=True)


def route(aff_t, cap):
    e, s = aff_t.shape
    nb = s // LANES
    a3 = aff_t.reshape(e, nb, LANES)
    idx, gate, pos, lo = pl.pallas_call(
        functools.partial(_route_kernel, cap=cap),
        grid=(e,),
        in_specs=[pl.BlockSpec((1, nb, LANES), lambda i: (i, 0, 0))],
        out_specs=[
            pl.BlockSpec((1, cap, 1), lambda i: (i, 0, 0)),
            pl.BlockSpec((1, cap, 1), lambda i: (i, 0, 0)),
            pl.BlockSpec((1, nb, LANES), lambda i: (i, 0, 0)),
            pl.BlockSpec((1, nb, LANES), lambda i: (i, 0, 0)),
        ],
        out_shape=[
            jax.ShapeDtypeStruct((e, cap, 1), I32),
            jax.ShapeDtypeStruct((e, cap, 1), F32),
            jax.ShapeDtypeStruct((e, nb, LANES), I32),
            jax.ShapeDtypeStruct((e, nb, LANES), I32),
        ],
        compiler_params=_params("parallel"),
        name="route",
    )(a3)
    return idx[:, :, 0], gate[:, :, 0], pos.reshape(e, s), lo[:, :, 0]


def _gather_kernel(idx_ref, src_ref, o_ref, buf, sem, *, rows):
    base = pl.program_id(0) * rows

    def issue(r, carry):
        pltpu.make_async_copy(src_ref.at[pl.ds(idx_ref[base + r], 1), :], buf.at[pl.ds(r, 1), :], sem).start()
        return carry

    lax.fori_loop(0, rows, issue, 0)
    pltpu.make_async_copy(src_ref.at[pl.ds(0, rows), :], buf, sem).wait()
    o_ref[...] = buf[...]


def gather_rows(src, gidx):
    p = gidx.shape[0]
    c = src.shape[1]
    rows = _tile(p, 256)
    return pl.pallas_call(
        functools.partial(_gather_kernel, rows=rows),
        grid_spec=pltpu.PrefetchScalarGridSpec(
            num_scalar_prefetch=1,
            grid=(p // rows,),
            in_specs=[pl.BlockSpec(memory_space=pl.ANY)],
            out_specs=pl.BlockSpec((rows, c), lambda i, idx: (i, 0)),
            scratch_shapes=[pltpu.VMEM((rows, c), src.dtype), pltpu.SemaphoreType.DMA(())],
        ),
        out_shape=jax.ShapeDtypeStruct((p, c), src.dtype),
        compiler_params=pltpu.CompilerParams(dimension_semantics=("arbitrary",), vmem_limit_bytes=VMEM_LIMIT,
                                             disable_bounds_checks=True),
        name="gather_rows",
    )(gidx, src)


def _ffn_up_kernel(x_ref, wg_ref, wu_ref, o_ref, xb_ref):
    @pl.when(pl.program_id(2) == 0)
    def _():
        xb_ref[...] = _unpack_halves(x_ref[...])

    x = xb_ref[...]
    a = jnp.dot(x, wg_ref[...], preferred_element_type=F32)
    b = jnp.dot(x, wu_ref[...], preferred_element_type=F32)
    o_ref[...] = (a * _sigmoid(a) * b).astype(o_ref.dtype)


def ffn_up(xe, w_gate, w_up, layer, n_exp):
    p, c = xe.shape
    d, f = w_gate.shape[-2:]
    capt = p // n_exp
    tm, tn = _tile(capt, 1024, 16), _tile(f, 512, LANES)
    nt = capt // tm
    return pl.pallas_call(
        _ffn_up_kernel,
        grid=(n_exp, nt, f // tn),
        in_specs=[
            pl.BlockSpec((tm, c), lambda e, i, j: (e * nt + i, 0)),
            pl.BlockSpec((None, None, d, tn), lambda e, i, j: (layer, e, 0, j)),
            pl.BlockSpec((None, None, d, tn), lambda e, i, j: (layer, e, 0, j)),
        ],
        out_specs=pl.BlockSpec((tm, tn), lambda e, i, j: (e * nt + i, j)),
        out_shape=jax.ShapeDtypeStruct((p, f), BF16),
        scratch_shapes=[pltpu.VMEM((tm, d), BF16)],
        compiler_params=_params("parallel", "parallel", "arbitrary"),
        name="ffn_up",
    )(xe, w_gate, w_up)


def _ffn_down_kernel(h_ref, w_ref, g_ref, o_ref):
    y = jnp.dot(h_ref[...], w_ref[...], preferred_element_type=F32) * g_ref[...]
    o_ref[...] = y.astype(o_ref.dtype)


def ffn_down(hid, w_down, gate_col, layer, n_exp):
    p, f = hid.shape
    d = w_down.shape[-1]
    capt = p // n_exp
    tm, tn = _tile(capt, 1024, 16), _tile(d, 1024, LANES)
    nt = capt // tm
    return pl.pallas_call(
        _ffn_down_kernel,
        grid=(n_exp, nt, d // tn),
        in_specs=[
            pl.BlockSpec((tm, f), lambda e, i, j: (e * nt + i, 0)),
            pl.BlockSpec((None, None, f, tn), lambda e, i, j: (layer, e, 0, j)),
            pl.BlockSpec((tm, 1), lambda e, i, j: (e * nt + i, 0)),
        ],
        out_specs=pl.BlockSpec((tm, tn), lambda e, i, j: (e * nt + i, j)),
        out_shape=jax.ShapeDtypeStruct((p, d), BF16),
        compiler_params=_params("parallel", "parallel", "parallel"),
        name="ffn_down",
    )(hid, w_down, gate_col)


SLAB = 64


def _combine_kernel(lo_ref, hi_ref, base_ref, h_ref, pos_ref, y_ref, o_ref, buf, xbuf, sem, xsem,
                    *, n_exp, bpt, p_rows):
    i, j = pl.program_id(0), pl.program_id(1)
    ni, nj = pl.num_programs(0), pl.num_programs(1)
    nblk = lo_ref.shape[0] // n_exp
    tt, dc = h_ref.shape
    step = i * nj + j
    slot = step % 2

    def window(tile, e):
        base = base_ref[e * nblk + tile * bpt]
        lo_al = (lo_ref[e * nblk + tile * bpt] // 16) * 16
        row = jnp.minimum(base + lo_al, p_rows - SLAB)
        return row - base, pl.multiple_of(row, 16)

    def fetch(tile, col, to_slot):
        for e in range(n_exp):
            _, row = window(tile, e)
            pltpu.make_async_copy(y_ref.at[pl.ds(row, SLAB), pl.ds(pl.multiple_of(col * dc, LANES), dc)],
                                  buf.at[to_slot, pl.ds(e * SLAB, SLAB), :], sem.at[to_slot]).start()

    @pl.when(step == 0)
    def _():
        fetch(0, 0, 0)

    pltpu.make_async_copy(y_ref.at[pl.ds(0, n_exp * SLAB), pl.ds(0, dc)], buf.at[slot], sem.at[slot]).wait()

    @pl.when(step + 1 < ni * nj)
    def _():
        nxt = step + 1
        fetch(nxt // nj, nxt % nj, 1 - slot)

    jlane = lax.broadcasted_iota(I32, (tt, SLAB), 1)
    pieces = []
    for e in range(n_exp):
        first, _ = window(i, e)
        pieces.append(jnp.where(pos_ref[:, e:e + 1] - first == jlane, 1.0, 0.0))
    sel = jnp.concatenate(pieces, axis=1).astype(BF16)
    o_ref[...] = h_ref[...] + jnp.dot(sel, buf[slot], preferred_element_type=F32)

    for e in range(n_exp):
        first, _ = window(i, e)
        hi = hi_ref[e * nblk + i * bpt + bpt - 1]
        base = base_ref[e * nblk + i * bpt]
        pos = pos_ref[:, e:e + 1]

        def extra(c, carry):
            start = first + (c + 1) * SLAB
            row = jnp.minimum(base + start, p_rows - SLAB)
            cp = pltpu.make_async_copy(y_ref.at[pl.ds(pl.multiple_of(row, 16), SLAB),
                                                pl.ds(pl.multiple_of(j * dc, LANES), dc)], xbuf, xsem)
            cp.start()
            cp.wait()
            more = ((pos - (row - base) == jlane) & (pos >= start)).astype(BF16)
            o_ref[...] += jnp.dot(more, xbuf[...], preferred_element_type=F32)
            return carry

        lax.fori_loop(0, jnp.maximum(hi - first - 1, 0) // SLAB, extra, 0)


def combine(h, pos_t, y, lo_flat, hi_flat, base_flat, n_exp):
    n, d = h.shape
    tt = _tile(n, 256, LANES)
    dc = d
    return pl.pallas_call(
        functools.partial(_combine_kernel, n_exp=n_exp, bpt=tt // LANES, p_rows=y.shape[0]),
        grid_spec=pltpu.PrefetchScalarGridSpec(
            num_scalar_prefetch=3,
            grid=(n // tt, d // dc),
            in_specs=[
                pl.BlockSpec((tt, dc), lambda i, j, *_: (i, j)),
                pl.BlockSpec((tt, LANES), lambda i, j, *_: (i, 0)),
                pl.BlockSpec(memory_space=pl.ANY),
            ],
            out_specs=pl.BlockSpec((tt, dc), lambda i, j, *_: (i, j)),
            scratch_shapes=[pltpu.VMEM((2, n_exp * SLAB, dc), BF16), pltpu.VMEM((SLAB, dc), BF16),
                            pltpu.SemaphoreType.DMA((2,)), pltpu.SemaphoreType.DMA(())],
        ),
        out_shape=jax.ShapeDtypeStruct((n, d), F32),
        compiler_params=pltpu.CompilerParams(dimension_semantics=("arbitrary", "arbitrary"),
                                             vmem_limit_bytes=VMEM_LIMIT, disable_bounds_checks=True),
        name="combine",
    )(lo_flat, hi_flat, base_flat, h, pos_t, y)


def expert_choice_ffn(h, norm_g, w_router_pad, w_gate, w_up, w_down, layer, seqs, cfg):
    n, d = h.shape
    ne = cfg.n_experts
    hn, aff = norm_proj(h, norm_g, w_router_pad, jnp.zeros((1, LANES), F32), eps=cfg.eps, softmax_cols=ne,
                        packed=True)
    aff_t = aff[:, :ne].T
    caps = [cfg.ec_capacity * s // ne for _, s in seqs]
    capt = sum(caps)
    idx_l, gate_l, pos_l, lo_l, hi_l, base_l = [], [], [], [], [], []
    row0 = 0
    for (off, s), cap in zip(seqs, caps):
        idx, gate, pos, lo = route(aff_t[:, off:off + s], cap)
        idx_l.append(idx + off)
        gate_l.append(gate)
        pos_l.append(pos)
        lo_l.append(lo)
        hi_l.append(jnp.concatenate([lo[:, 1:], jnp.full((ne, 1), cap, I32)], axis=1))
        base_l.append(jnp.broadcast_to((jnp.arange(ne, dtype=I32) * capt + row0)[:, None], lo.shape))
        row0 += cap
    gidx = jnp.concatenate(idx_l, axis=1).reshape(-1)
    gate_col = jnp.concatenate(gate_l, axis=1).reshape(-1, 1)
    pos_t = _pad_cols(jnp.concatenate(pos_l, axis=1).T)
    lo_flat, hi_flat, base_flat = (jnp.concatenate(t, axis=1).reshape(-1) for t in (lo_l, hi_l, base_l))

    xe = gather_rows(hn, gidx)
    hid = ffn_up(xe, w_gate, w_up, layer, ne)
    y = ffn_down(hid, w_down, gate_col, layer, ne)
    return combine(h, pos_t, y, lo_flat, hi_flat, base_flat, ne)


def _pad_cols(w, cols=LANES):
    return jnp.pad(w, [(0, 0)] * (w.ndim - 1) + [(0, cols - w.shape[-1])])


def trunk(x, seqs, p, cfg):
    n, d = x.shape
    nh = cfg.ml_heads
    depth = p["w_in"].shape[0]
    o_g = 3 * cfg.naw + 2 * cfg.qkw + 2 * cfg.vw
    w_in = p["w_in"]
    w_main = jnp.concatenate([w_in[:, :, :o_g], w_in[:, :, o_g + 4 * nh:]], axis=-1).astype(BF16)
    w_gates = _pad_cols(w_in[:, :, o_g:o_g + 4 * nh]).astype(BF16)
    b_gates = _pad_cols(p["b_gate"])[:, None, :]
    w_pa, w_pm, w_out = (p[k].astype(BF16) for k in ("w_pa", "w_pm", "w_out"))
    w_router = _pad_cols(p["w_router"]).astype(BF16)
    w_fg, w_fu, w_fd = (p[k].astype(BF16) for k in ("w_ff_gate", "w_ff_up", "w_ff_down"))
    na_bias = na_bias_table(p["rpb"], cfg)
    col_scale = jnp.concatenate([jnp.full((1, cfg.qkw), cfg.ml_qk ** -0.5, F32), jnp.ones((1, cfg.qkw), F32)], axis=1)

    for l in range(depth):
        xn, gates = norm_proj(x, p["norm1_g"][l][None], w_gates[l], b_gates[l], eps=cfg.eps)
        z = in_proj(xn, w_main, l)
        ya = neighborhood_attention(z, na_bias[l], p["q_gain"][l][None], p["k_gain"][l][None], seqs, cfg)
        mqk = conv_silu(z, p["conv_w"][l], p["conv_b"][l][None], col_scale, seqs, cfg)
        kt = mqk[:, cfg.qkw:].T
        g_dir = jnp.stack([_pad_cols(gates[:, 0:2 * nh]), _pad_cols(gates[:, 2 * nh:4 * nh])])
        pc = gate_cumsum(g_dir, cfg)
        pr = jnp.transpose(pc[:, :, :2 * nh], (0, 2, 1))
        hdir = mlstm(mqk, kt, z, pc, pr, seqs, cfg)
        ym = ml_out(hdir, z, p["ml_gain"][l].reshape(1, -1), cfg)
        merged = merge_proj(ya, ym, w_pa, w_pm, z, l, cfg)
        h = out_proj(merged, w_out, x, l)
        x = expert_choice_ffn(h, p["norm2_g"][l][None], w_router[l], w_fg, w_fu, w_fd, l, seqs, cfg)
    return x


def kernel(x_prompt, x_sample, norm1_g, norm2_g, w_in, b_gate, conv_w, conv_b, q_gain, k_gain, rpb, ml_gain,
           w_pa, w_pm, w_out, w_router, w_ff_gate, w_ff_up, w_ff_down):
    cfg = CFG
    d = x_prompt.shape[-1]
    xs = [x_prompt.reshape(-1, d), x_sample.reshape(-1, d)]
    seqs, off = [], 0
    for xx in (x_prompt, x_sample):
        for _ in range(xx.shape[0]):
            seqs.append((off, xx.shape[1]))
            off += xx.shape[1]
    p = dict(norm1_g=norm1_g, norm2_g=norm2_g, w_in=w_in, b_gate=b_gate, conv_w=conv_w, conv_b=conv_b,
             q_gain=q_gain, k_gain=k_gain, rpb=rpb, ml_gain=ml_gain, w_pa=w_pa, w_pm=w_pm, w_out=w_out,
             w_router=w_router, w_ff_gate=w_ff_gate, w_ff_up=w_ff_up, w_ff_down=w_ff_down)
    y = trunk(jnp.concatenate(xs, axis=0), seqs, p, cfg)
    n_p = xs[0].shape[0]
    return (y[:n_p].reshape(x_prompt.shape), y[n_p:].reshape(x_sample.shape))
```

```python
import dataclasses
import functools

import jax
import jax.numpy as jnp
import numpy as np
from jax import lax
from jax.experimental import pallas as pl
from jax.experimental.pallas import tpu as pltpu

F32 = jnp.float32
BF16 = jnp.bfloat16
I32 = jnp.int32
NEG = -1e30
LANES = 128
VMEM_LIMIT = 56 * 1024 * 1024


@dataclasses.dataclass(frozen=True)
class Cfg:
    d_model: int = 4096
    grid_w: int = 64
    na_heads: int = 16
    na_dim: int = 128
    na_rows: int = 8
    na_cols: int = 16
    ml_heads: int = 8
    ml_qk: int = 128
    ml_v: int = 256
    ml_conv: int = 5
    n_experts: int = 16
    ec_capacity: int = 2
    d_ff: int = 2048
    eps: float = 1e-6
    ml_chunk: int = 256

    @property
    def naw(self):
        return self.na_heads * self.na_dim

    @property
    def qkw(self):
        return self.ml_heads * self.ml_qk

    @property
    def vw(self):
        return self.ml_heads * self.ml_v

    @property
    def z_na_k(self):
        return self.naw

    @property
    def z_na_v(self):
        return 2 * self.naw

    @property
    def z_ml_q(self):
        return 3 * self.naw

    @property
    def z_ml_v(self):
        return self.z_ml_q + 2 * self.qkw

    @property
    def z_ml_o(self):
        return self.z_ml_v + self.vw

    @property
    def z_merge(self):
        return self.z_ml_o + self.vw

    @property
    def z_cols(self):
        return self.z_merge + 2 * self.d_model


CFG = Cfg()


def _tile(dim, pref, mult=8):
    if dim <= pref:
        return dim
    t = (pref // mult) * mult
    while t > mult and dim % t:
        t -= mult
    assert dim % t == 0, (dim, pref, mult)
    return t


def _params(*sem):
    return pltpu.CompilerParams(dimension_semantics=sem, vmem_limit_bytes=VMEM_LIMIT)


def _sigmoid(x):
    return 1.0 / (1.0 + jnp.exp(-x))


def _pack_halves(x):
    half = x.shape[1] // 2
    lo = lax.bitcast_convert_type(x[:, :half].astype(F32), jnp.uint32)
    hi = lax.bitcast_convert_type(x[:, half:].astype(F32), jnp.uint32)
    return (lo >> 16) | (hi & jnp.uint32(0xFFFF0000))


def _unpack_halves(w):
    lo = lax.bitcast_convert_type(w << 16, F32).astype(BF16)
    hi = lax.bitcast_convert_type(w & jnp.uint32(0xFFFF0000), F32).astype(BF16)
    return jnp.concatenate([lo, hi], axis=1)


def _norm_proj_kernel(x_ref, g_ref, w_ref, b_ref, xn_ref, s_ref, *, eps, softmax_cols, packed):
    x = x_ref[...]
    ms = jnp.mean(x * x, axis=-1, keepdims=True)
    xn = (x * lax.rsqrt(ms + eps) * g_ref[...]).astype(BF16)
    xn_ref[...] = _pack_halves(xn) if packed else xn
    s = jnp.dot(xn, w_ref[...], preferred_element_type=F32) + b_ref[...]
    if softmax_cols:
        lane = lax.broadcasted_iota(I32, s.shape, 1)
        s = jnp.where(lane < softmax_cols, s, NEG)
        e = jnp.exp(s - jnp.max(s, axis=-1, keepdims=True))
        s = e / jnp.sum(e, axis=-1, keepdims=True)
    s_ref[...] = s


def norm_proj(x, g, w_small, b_small, *, eps, softmax_cols=0, packed=False):
    n, d = x.shape
    tm = _tile(n, 512)
    xn_t = jax.eval_shape(_pack_halves, jax.ShapeDtypeStruct((n, d), BF16)) if packed else jax.ShapeDtypeStruct((n, d), BF16)
    return pl.pallas_call(
        functools.partial(_norm_proj_kernel, eps=eps, softmax_cols=softmax_cols, packed=packed),
        grid=(n // tm,),
        in_specs=[
            pl.BlockSpec((tm, d), lambda i: (i, 0)),
            pl.BlockSpec((1, d), lambda i: (0, 0)),
            pl.BlockSpec((d, LANES), lambda i: (0, 0)),
            pl.BlockSpec((1, LANES), lambda i: (0, 0)),
        ],
        out_specs=[
            pl.BlockSpec((tm, xn_t.shape[1]), lambda i: (i, 0)),
            pl.BlockSpec((tm, LANES), lambda i: (i, 0)),
        ],
        out_shape=[xn_t, jax.ShapeDtypeStruct((n, LANES), F32)],
        compiler_params=_params("parallel"),
        name="norm_proj",
    )(x, g, w_small, b_small)


def _mm_kernel(x_ref, w_ref, o_ref):
    o_ref[...] = jnp.dot(x_ref[...], w_ref[...], preferred_element_type=F32).astype(o_ref.dtype)


def in_proj(xn, w_main, layer, tm_pref=1024, tn_pref=1024, x_buffers=2):
    n, d = xn.shape
    zc = w_main.shape[-1]
    tm, tn = _tile(n, tm_pref), _tile(zc, tn_pref, LANES)
    return pl.pallas_call(
        _mm_kernel,
        grid=(n // tm, zc // tn),
        in_specs=[
            pl.BlockSpec((tm, d), lambda i, j: (i, 0), pipeline_mode=pl.Buffered(x_buffers)),
            pl.BlockSpec((None, d, tn), lambda i, j: (layer, 0, j)),
        ],
        out_specs=pl.BlockSpec((tm, tn), lambda i, j: (i, j)),
        out_shape=jax.ShapeDtypeStruct((n, zc), BF16),
        compiler_params=_params("parallel", "parallel"),
        name="in_proj",
    )(xn, w_main)


def _merge_kernel(ya_ref, ym_ref, wa_ref, wm_ref, ga_ref, gm_ref, o_ref):
    a = jnp.dot(ya_ref[...], wa_ref[...], preferred_element_type=F32)
    m = jnp.dot(ym_ref[...], wm_ref[...], preferred_element_type=F32)
    o = _sigmoid(ga_ref[...].astype(F32)) * a + _sigmoid(gm_ref[...].astype(F32)) * m
    o_ref[...] = o.astype(o_ref.dtype)


def merge_proj(ya, ym, w_pa, w_pm, z, layer, cfg):
    n = ya.shape[0]
    d = cfg.d_model
    tm, tn = _tile(n, 1024), _tile(d, 1024, LANES)
    assert cfg.z_merge % tn == 0
    ga0 = cfg.z_merge // tn
    gm0 = (cfg.z_merge + d) // tn
    return pl.pallas_call(
        _merge_kernel,
        grid=(n // tm, d // tn),
        in_specs=[
            pl.BlockSpec((tm, cfg.naw), lambda i, j: (i, 0)),
            pl.BlockSpec((tm, cfg.vw), lambda i, j: (i, 0)),
            pl.BlockSpec((None, cfg.naw, tn), lambda i, j: (layer, 0, j)),
            pl.BlockSpec((None, cfg.vw, tn), lambda i, j: (layer, 0, j)),
            pl.BlockSpec((tm, tn), lambda i, j: (i, ga0 + j)),
            pl.BlockSpec((tm, tn), lambda i, j: (i, gm0 + j)),
        ],
        out_specs=pl.BlockSpec((tm, tn), lambda i, j: (i, j)),
        out_shape=jax.ShapeDtypeStruct((n, d), BF16),
        compiler_params=_params("parallel", "parallel"),
        name="merge_proj",
    )(ya, ym, w_pa, w_pm, z, z)


def _out_proj_kernel(m_ref, w_ref, r_ref, o_ref):
    o_ref[...] = r_ref[...] + jnp.dot(m_ref[...], w_ref[...], preferred_element_type=F32)


def out_proj(merged, w_out, res, layer):
    n, d = merged.shape
    tm, tn = _tile(n, 1024), _tile(d, 1024, LANES)
    return pl.pallas_call(
        _out_proj_kernel,
        grid=(n // tm, d // tn),
        in_specs=[
            pl.BlockSpec((tm, d), lambda i, j: (i, 0)),
            pl.BlockSpec((None, d, tn), lambda i, j: (layer, 0, j)),
            pl.BlockSpec((tm, tn), lambda i, j: (i, j)),
        ],
        out_specs=pl.BlockSpec((tm, tn), lambda i, j: (i, j)),
        out_shape=jax.ShapeDtypeStruct((n, d), F32),
        compiler_params=_params("parallel", "parallel"),
        name="out_proj",
    )(merged, w_out, res)


NA_GROUP = 4
NA_CHUNK_ROWS = 64
NA_UNROLL = 16


def na_bias_table(rpb, cfg):
    w_, kr, nc, gr = cfg.grid_w, cfg.na_rows, cfg.na_cols, NA_GROUP
    ur = kr + gr
    assert gr <= kr // 2
    cols = np.arange(w_)
    col0 = np.clip(cols - nc // 2, 0, w_ - nc)
    valid_c = (cols[None, :] >= col0[:, None]) & (cols[None, :] < col0[:, None] + nc)
    dc = cols[None, :] - cols[:, None] + (nc - 1)
    cm = (np.arange(2 * nc - 1)[:, None, None] == dc[None]) & valid_c[None]
    j = np.arange(gr)
    off = np.stack([0 * j, j, 0 * j + (ur - kr)])
    delta = np.stack([j, 0 * j + kr // 2, kr - gr + j])
    k = np.arange(ur)[None, None, :] - off[:, :, None]
    valid_r = (k >= 0) & (k < kr)
    dr = k - delta[:, :, None] + (kr - 1)
    rm = (np.arange(2 * kr - 1)[:, None, None, None] == dr[None]) & valid_r[None]
    t = jnp.einsum("lhab,atju,bwc->lhtjwuc", rpb.astype(F32), rm.astype(np.float32), cm.astype(np.float32),
                   precision=lax.Precision.HIGHEST)
    valid = valid_r[:, :, None, :, None] & valid_c[None, None, :, None, :]
    t = jnp.where(valid[None, None], t, NEG)
    return t.reshape(rpb.shape[0], rpb.shape[1], 3, gr * w_, ur * w_)


def _na_kernel(q_ref, k_ref, v_ref, b_ref, qg_ref, kg_ref, o_ref, kn_ref, vx_ref, *, row_bounds, chunk_rows, cfg):
    w_, kr, gr = cfg.grid_w, cfg.na_rows, NA_GROUP
    band = kr * w_
    gq = gr * w_
    un = (kr + gr) * w_
    eps = cfg.eps
    c = pl.program_id(1)
    n_rows = k_ref.shape[0] // w_

    @pl.when(c == 0)
    def _():
        kg = kg_ref[...]

        ones_col = jnp.where(lax.broadcasted_iota(I32, (band, LANES), 1) == 0, 1.0, 0.0).astype(BF16)

        def knorm(i, carry):
            sl = pl.ds(pl.multiple_of(i * band, band), band)
            kk = k_ref[sl, :].astype(F32)
            ms = jnp.mean(kk * kk, axis=-1, keepdims=True)
            kn_ref[sl, :] = (kk * lax.rsqrt(ms + eps) * kg).astype(BF16)
            vx_ref[sl, :] = jnp.concatenate([v_ref[sl, :], ones_col], axis=1)
            return carry

        lax.fori_loop(0, n_rows // kr, knorm, 0)

    qg = qg_ref[...] * (cfg.na_dim ** -0.5)
    g0 = c * chunk_rows
    lo = functools.reduce(lambda acc, b: jnp.where(g0 >= b[0], b[0], acc), row_bounds, 0)
    hi = functools.reduce(lambda acc, b: jnp.where(g0 >= b[0], b[1], acc), row_bounds, 0)

    def group(i, carry):
        first = g0 + i * gr
        u0 = jnp.clip(first - kr // 2, lo, hi - (kr + gr))
        kind = jnp.where(first == lo, 0, jnp.where(first == hi - gr, 2, 1))
        qsl = pl.ds(pl.multiple_of(i * gq, gq), gq)
        usl = pl.ds(pl.multiple_of(u0 * w_, w_), un)
        q = q_ref[qsl, :].astype(F32)
        ms = jnp.mean(q * q, axis=-1, keepdims=True)
        qn = (q * lax.rsqrt(ms + eps) * qg).astype(BF16)
        s = lax.dot_general(qn, kn_ref[usl, :], (((1,), (1,)), ((), ())), preferred_element_type=F32)
        s = s + b_ref[0, kind]
        e = jnp.exp((s - jnp.max(s, axis=-1, keepdims=True)).astype(BF16))
        o = jnp.dot(e, vx_ref[usl, :], preferred_element_type=F32)
        o_ref[qsl, :] = (o[:, :LANES] / o[:, LANES:LANES + 1]).astype(o_ref.dtype)
        return carry

    lax.fori_loop(0, chunk_rows // gr, group, 0, unroll=min(NA_UNROLL, chunk_rows // gr))


def neighborhood_attention(z, bias, q_gain, k_gain, seqs, cfg):
    n = z.shape[0]
    hd = cfg.na_dim
    w_ = cfg.grid_w
    chunk_rows = functools.reduce(np.gcd, [s // w_ for _, s in seqs] + [o // w_ for o, _ in seqs] + [NA_CHUNK_ROWS])
    cq = int(chunk_rows) * w_
    gr, ur = NA_GROUP, cfg.na_rows + NA_GROUP
    assert hd == LANES and all(o % cq == 0 and s % cq == 0 and s // w_ >= ur for o, s in seqs)
    assert chunk_rows % gr == 0 and (n // w_) % cfg.na_rows == 0
    row_bounds = [(o // w_, (o + s) // w_) for o, s in seqs]
    kq, kk, kv = 0, cfg.z_na_k // hd, cfg.z_na_v // hd
    return pl.pallas_call(
        functools.partial(_na_kernel, row_bounds=row_bounds, chunk_rows=int(chunk_rows), cfg=cfg),
        grid=(cfg.na_heads, n // cq),
        in_specs=[
            pl.BlockSpec((cq, hd), lambda h, c: (c, kq + h)),
            pl.BlockSpec((n, hd), lambda h, c: (0, kk + h), pipeline_mode=pl.Buffered(1)),
            pl.BlockSpec((n, hd), lambda h, c: (0, kv + h), pipeline_mode=pl.Buffered(1)),
            pl.BlockSpec((1, 3, gr * w_, ur * w_), lambda h, c: (h, 0, 0, 0)),
            pl.BlockSpec((1, hd), lambda h, c: (0, 0)),
            pl.BlockSpec((1, hd), lambda h, c: (0, 0)),
        ],
        out_specs=pl.BlockSpec((cq, hd), lambda h, c: (c, h)),
        out_shape=jax.ShapeDtypeStruct((n, cfg.naw), BF16),
        scratch_shapes=[pltpu.VMEM((n, hd), BF16), pltpu.VMEM((n, 2 * hd), BF16)],
        compiler_params=_params("arbitrary", "arbitrary"),
        name="neighborhood_attention",
    )(z, z, z, bias, q_gain, k_gain)


HALO = 16


def _conv_kernel(x_ref, p_ref, n_ref, w_ref, b_ref, s_ref, o_ref, *, ts, taps, starts, ends):
    i = pl.program_id(1)
    is_start = functools.reduce(jnp.logical_or, [i == s // ts for s in starts])
    is_end = functools.reduce(jnp.logical_or, [i == e // ts - 1 for e in ends])
    xp = p_ref[...].astype(F32) * jnp.where(is_start, 0.0, 1.0)
    xn = n_ref[...].astype(F32) * jnp.where(is_end, 0.0, 1.0)
    ext = jnp.concatenate([xp, x_ref[...].astype(F32), xn], axis=0)
    w = w_ref[...]
    half = taps // 2
    acc = jnp.zeros((ts, x_ref.shape[1]), F32) + b_ref[...]
    for j in range(taps):
        lo = HALO + j - half
        acc = acc + w[j:j + 1, :] * ext[lo:lo + ts, :]
    o_ref[...] = (acc * _sigmoid(acc) * s_ref[...]).astype(o_ref.dtype)


def conv_silu(z, conv_w, conv_b, col_scale, seqs, cfg):
    n = z.shape[0]
    c = 2 * cfg.qkw
    cb = _tile(c, 512, LANES)
    ts = _tile(min(s for _, s in seqs), 1024, HALO)
    assert all(o % ts == 0 and s % ts == 0 for o, s in seqs) and cfg.z_ml_q % cb == 0
    c0 = cfg.z_ml_q // cb
    nh = n // HALO
    starts = [o for o, _ in seqs]
    ends = [o + s for o, s in seqs]
    return pl.pallas_call(
        functools.partial(_conv_kernel, ts=ts, taps=cfg.ml_conv, starts=starts, ends=ends),
        grid=(c // cb, n // ts),
        in_specs=[
            pl.BlockSpec((ts, cb), lambda j, i: (i, c0 + j)),
            pl.BlockSpec((HALO, cb), lambda j, i: (jnp.maximum(i * (ts // HALO) - 1, 0), c0 + j)),
            pl.BlockSpec((HALO, cb), lambda j, i: (jnp.minimum((i + 1) * (ts // HALO), nh - 1), c0 + j)),
            pl.BlockSpec((cfg.ml_conv, cb), lambda j, i: (0, j)),
            pl.BlockSpec((1, cb), lambda j, i: (0, j)),
            pl.BlockSpec((1, cb), lambda j, i: (0, j)),
        ],
        out_specs=pl.BlockSpec((ts, cb), lambda j, i: (i, j)),
        out_shape=jax.ShapeDtypeStruct((n, c), BF16),
        compiler_params=_params("parallel", "parallel"),
        name="conv_silu",
    )(z, z, z, conv_w, conv_b, col_scale)


def _split3(x):
    hi = x.astype(BF16)
    r = x - hi.astype(F32)
    mid = r.astype(BF16)
    lo = (r - mid.astype(F32)).astype(BF16)
    return hi, mid, lo


def _gate_kernel(g_ref, o_ref, *, nh):
    d = pl.program_id(0)
    g = g_ref[...]
    ell = g.shape[0]
    lf = jnp.minimum(g, 0.0) - jnp.log(1.0 + jnp.exp(-jnp.abs(g)))
    t = lax.broadcasted_iota(I32, (ell, ell), 0)
    u = lax.broadcasted_iota(I32, (ell, ell), 1)
    tri = jnp.where((u - t) * jnp.where(d == 0, 1, -1) <= 0, 1.0, 0.0).astype(BF16)
    cs = sum(jnp.dot(tri, p, preferred_element_type=F32) for p in _split3(lf))
    lane = lax.broadcasted_iota(I32, g.shape, 1)
    o_ref[...] = jnp.where((lane >= nh) & (lane < 2 * nh), cs, g)


def gate_cumsum(g_dir, cfg):
    _, n, _ = g_dir.shape
    ell = cfg.ml_chunk
    return pl.pallas_call(
        functools.partial(_gate_kernel, nh=cfg.ml_heads),
        grid=(2, n // ell),
        in_specs=[pl.BlockSpec((None, ell, LANES), lambda d, c: (d, c, 0))],
        out_specs=pl.BlockSpec((None, ell, LANES), lambda d, c: (d, c, 0)),
        out_shape=jax.ShapeDtypeStruct(g_dir.shape, F32),
        compiler_params=_params("parallel", "parallel"),
        name="gate_cumsum",
    )(g_dir)


def _mlstm_kernel(q_ref, kt_ref, v_ref, pc_ref, pr_ref, o_ref, c_ref, m_ref, *, nc, firsts, lasts, cfg):
    d = pl.program_id(0)
    c = pl.program_id(1)
    ci = jnp.where(d == 0, c, nc - 1 - c)
    nh, dk, dv, ell = cfg.ml_heads, cfg.ml_qk, cfg.ml_v, cfg.ml_chunk
    dve = dv + LANES

    fwd_reset = functools.reduce(jnp.logical_or, [ci == f for f in firsts])
    bwd_reset = functools.reduce(jnp.logical_or, [ci == f for f in lasts])

    @pl.when(jnp.where(d == 0, fwd_reset, bwd_reset))
    def _():
        c_ref[...] = jnp.zeros_like(c_ref)
        m_ref[...] = jnp.zeros_like(m_ref)

    t = lax.broadcasted_iota(I32, (ell, ell), 0)
    s = lax.broadcasted_iota(I32, (ell, ell), 1)
    tri = (s - t) * jnp.where(d == 0, 1, -1) <= 0
    ones_col = jnp.where(lax.broadcasted_iota(I32, (ell, LANES), 1) == 0, 1.0, 0.0).astype(BF16)
    pc = pc_ref[...]
    pr = pr_ref[...]

    for h in range(nh):
        qh = q_ref[:, h * dk:(h + 1) * dk]
        kth = kt_ref[h * dk:(h + 1) * dk, :]
        vext = jnp.concatenate([v_ref[:, h * dv:(h + 1) * dv], ones_col], axis=1)
        bcol = pc[:, nh + h:nh + h + 1]
        irow = pr[h:h + 1, :]
        brow = pr[nh + h:nh + h + 1, :]
        tot = jnp.where(d == 0, brow[:, ell - 1:ell], brow[:, 0:1])
        m_prev = m_ref[h][0:1, 0:1]
        cext = c_ref[h]

        dm = jnp.where(tri, bcol - (brow - irow), NEG)
        inter = bcol + m_prev
        m_t = jnp.maximum(inter, jnp.max(dm, axis=1, keepdims=True))
        sc = jnp.exp(inter - m_t)
        a = jnp.dot(qh, kth, preferred_element_type=F32) * jnp.exp(dm - m_t)
        nd = jnp.dot(a.astype(BF16), vext, preferred_element_type=F32)
        nd = nd + sc * jnp.dot(qh, cext.astype(BF16), preferred_element_type=F32)
        den = jnp.maximum(jnp.abs(nd[:, dv:dv + 1]), jnp.exp(-m_t))
        o_ref[:, h * dv:(h + 1) * dv] = nd[:, :dv] / den

        dend = tot - brow + irow
        m_new = jnp.maximum(tot + m_prev, jnp.max(dend, axis=1, keepdims=True))
        wk = jnp.exp(dend - m_new)
        decay = jnp.exp(tot + m_prev - m_new)
        kw = (kth.astype(F32) * wk).astype(BF16)
        c_ref[h] = decay * cext + jnp.dot(kw, vext, preferred_element_type=F32)
        m_ref[h] = jnp.broadcast_to(m_new, m_ref.shape[1:])


def mlstm(mqk, kt, z, pc, pr, seqs, cfg):
    n = z.shape[0]
    ell = cfg.ml_chunk
    nc = n // ell
    assert all(o % ell == 0 and s % ell == 0 for o, s in seqs)
    firsts = [o // ell for o, _ in seqs]
    lasts = [(o + s) // ell - 1 for o, s in seqs]
    qkw, vw = cfg.qkw, cfg.vw
    assert cfg.z_ml_v % vw == 0
    v0 = cfg.z_ml_v // vw

    def cidx(d, c):
        return jnp.where(d == 0, c, nc - 1 - c)

    return pl.pallas_call(
        functools.partial(_mlstm_kernel, nc=nc, firsts=firsts, lasts=lasts, cfg=cfg),
        grid=(2, nc),
        in_specs=[
            pl.BlockSpec((ell, qkw), lambda d, c: (cidx(d, c), 0)),
            pl.BlockSpec((qkw, ell), lambda d, c: (0, cidx(d, c))),
            pl.BlockSpec((ell, vw), lambda d, c: (cidx(d, c), v0)),
            pl.BlockSpec((None, ell, LANES), lambda d, c: (d, cidx(d, c), 0)),
            pl.BlockSpec((None, 2 * cfg.ml_heads, ell), lambda d, c: (d, 0, cidx(d, c))),
        ],
        out_specs=pl.BlockSpec((None, ell, vw), lambda d, c: (d, cidx(d, c), 0)),
        out_shape=jax.ShapeDtypeStruct((2, n, vw), F32),
        scratch_shapes=[
            pltpu.VMEM((cfg.ml_heads, cfg.ml_qk, cfg.ml_v + LANES), F32),
            pltpu.VMEM((cfg.ml_heads, 8, LANES), F32),
        ],
        compiler_params=_params("arbitrary", "arbitrary"),
        name="mlstm",
    )(mqk, kt, z, pc, pr)


def _ml_out_kernel(hf_ref, hb_ref, og_ref, g_ref, o_ref, *, nh, dv, eps):
    for h in range(nh):
        sl = slice(h * dv, (h + 1) * dv)
        x = hf_ref[:, sl] + hb_ref[:, sl]
        ms = jnp.mean(x * x, axis=-1, keepdims=True)
        y = x * lax.rsqrt(ms + eps) * g_ref[:, sl]
        o_ref[:, sl] = (y * _sigmoid(og_ref[:, sl].astype(F32))).astype(o_ref.dtype)


def ml_out(hdir, z, ml_gain, cfg):
    _, n, vw = hdir.shape
    tm = _tile(n, 512)
    assert cfg.z_ml_o % vw == 0
    o0 = cfg.z_ml_o // vw
    return pl.pallas_call(
        functools.partial(_ml_out_kernel, nh=cfg.ml_heads, dv=cfg.ml_v, eps=cfg.eps),
        grid=(n // tm,),
        in_specs=[
            pl.BlockSpec((None, tm, vw), lambda i: (0, i, 0)),
            pl.BlockSpec((None, tm, vw), lambda i: (1, i, 0)),
            pl.BlockSpec((tm, vw), lambda i: (i, o0)),
            pl.BlockSpec((1, vw), lambda i: (0, 0)),
        ],
        out_specs=pl.BlockSpec((tm, vw), lambda i: (i, 0)),
        out_shape=jax.ShapeDtypeStruct((n, vw), BF16),
        compiler_params=_params("parallel"),
        name="ml_out",
    )(hdir, hdir, z, ml_gain)


def _prefix_counts(mask_f, upper, strict_lower):
    m16 = mask_f.astype(BF16)
    local = jnp.dot(m16, upper, preferred_element_type=F32)
    rowtot = jnp.broadcast_to(local[:, LANES - 1:LANES], local.shape)
    offs = jnp.dot(strict_lower, rowtot.astype(BF16), preferred_element_type=F32)
    return local + offs, offs


def _route_kernel(a_ref, idx_ref, gate_ref, pos_ref, lo_ref, *, cap):
    a = a_ref[0]
    nb = a.shape[0]
    bits = lax.bitcast_convert_type(a, I32)

    thr = jnp.zeros((1, 1), I32)
    for bit in range(30, -1, -1):
        cand = thr | (1 << bit)
        cnt = jnp.sum(jnp.where(bits >= cand, 1.0, 0.0), keepdims=True)
        thr = jnp.where(cnt >= cap, cand, thr)

    li = lax.broadcasted_iota(I32, (LANES, LANES), 0)
    lj = lax.broadcasted_iota(I32, (LANES, LANES), 1)
    upper = jnp.where(li <= lj, 1.0, 0.0).astype(BF16)
    bi = lax.broadcasted_iota(I32, (nb, nb), 0)
    bj = lax.broadcasted_iota(I32, (nb, nb), 1)
    strict_lower = jnp.where(bj < bi, 1.0, 0.0).astype(BF16)

    gt = bits > thr
    eq = bits == thr
    need = cap - jnp.sum(jnp.where(gt, 1.0, 0.0), keepdims=True)
    eq_f = jnp.where(eq, 1.0, 0.0)
    eq_incl, _ = _prefix_counts(eq_f, upper, strict_lower)
    sel = gt | (eq & (eq_incl <= need))
    sel_f = jnp.where(sel, 1.0, 0.0)
    incl, offs = _prefix_counts(sel_f, upper, strict_lower)

    pos_ref[0] = jnp.where(sel, incl - 1.0, -1.0).astype(I32)
    lo_ref[0] = offs.astype(I32)

    p_col = lax.broadcasted_iota(I32, (cap, 1), 0).astype(F32)
    ones8 = jnp.ones((8, LANES), BF16)
    cnt_row = lax.dot_general(ones8, sel_f.astype(BF16), (((1,), (1,)), ((), ())),
                              preferred_element_type=F32)[0:1, :]
    ui = lax.broadcasted_iota(I32, (nb, nb), 0)
    uj = lax.broadcasted_iota(I32, (nb, nb), 1)
    upper_nb = jnp.where(ui <= uj, 1.0, 0.0).astype(BF16)
    cum8 = jnp.dot(jnp.broadcast_to(cnt_row, (8, nb)).astype(BF16), upper_nb, preferred_element_type=F32)
    blockcum = cum8[0:1, :]
    before = blockcum <= p_col
    jblk = jnp.sum(jnp.where(before, 1.0, 0.0), axis=1, keepdims=True)
    base = jnp.sum(jnp.where(before, cnt_row, 0.0), axis=1, keepdims=True)
    onehot = (lax.broadcasted_iota(I32, (cap, nb), 1).astype(F32) == jblk).astype(BF16)
    local_incl = incl - offs
    a_hi, a_mid, a_lo = _split3(a)
    table = jnp.concatenate([local_incl.astype(BF16), a_hi, a_mid, a_lo], axis=1)
    rows = jnp.dot(onehot, table, preferred_element_type=F32)
    rank = p_col - base + 1.0
    lane_cnt = jnp.sum(jnp.where(rows[:, :LANES] < rank, 1.0, 0.0), axis=1, keepdims=True)
    idx_ref[0] = (jblk * LANES + lane_cnt).astype(I32)
    arow = rows[:, LANES:2 * LANES] + rows[:, 2 * LANES:3 * LANES] + rows[:, 3 * LANES:]
    lane = lax.broadcasted_iota(I32, (cap, LANES), 1).astype(F32)
    gate_ref[0] = jnp.sum(jnp.where(lane == lane_cnt, arow, 0.0), axis=1, keepdims=True)


def route(aff_t, cap):
    e, s = aff_t.shape
    nb = s // LANES
    a3 = aff_t.reshape(e, nb, LANES)
    idx, gate, pos, lo = pl.pallas_call(
        functools.partial(_route_kernel, cap=cap),
        grid=(e,),
        in_specs=[pl.BlockSpec((1, nb, LANES), lambda i: (i, 0, 0))],
        out_specs=[
            pl.BlockSpec((1, cap, 1), lambda i: (i, 0, 0)),
            pl.BlockSpec((1, cap, 1), lambda i: (i, 0, 0)),
            pl.BlockSpec((1, nb, LANES), lambda i: (i, 0, 0)),
            pl.BlockSpec((1, nb, LANES), lambda i: (i, 0, 0)),
        ],
        out_shape=[
            jax.ShapeDtypeStruct((e, cap, 1), I32),
            jax.ShapeDtypeStruct((e, cap, 1), F32),
            jax.ShapeDtypeStruct((e, nb, LANES), I32),
            jax.ShapeDtypeStruct((e, nb, LANES), I32),
        ],
        compiler_params=_params("parallel"),
        name="route",
    )(a3)
    return idx[:, :, 0], gate[:, :, 0], pos.reshape(e, s), lo[:, :, 0]


def _gather_kernel(idx_ref, src_ref, o_ref, buf, sem, *, rows):
    i = pl.program_id(0)
    slot = i % 2

    def fetch(step, to_slot):
        def issue(r, carry):
            pltpu.make_async_copy(src_ref.at[pl.ds(idx_ref[step * rows + r], 1), :],
                                  buf.at[to_slot, pl.ds(r, 1), :], sem.at[to_slot]).start()
            return carry

        lax.fori_loop(0, rows, issue, 0, unroll=8)

    @pl.when(i == 0)
    def _():
        fetch(0, 0)

    @pl.when(i + 1 < pl.num_programs(0))
    def _():
        fetch(i + 1, 1 - slot)

    pltpu.make_async_copy(src_ref.at[pl.ds(0, rows), :], buf.at[slot], sem.at[slot]).wait()
    o_ref[...] = buf[slot]


def gather_rows(src, gidx):
    p = gidx.shape[0]
    c = src.shape[1]
    rows = _tile(p, 256)
    return pl.pallas_call(
        functools.partial(_gather_kernel, rows=rows),
        grid_spec=pltpu.PrefetchScalarGridSpec(
            num_scalar_prefetch=1,
            grid=(p // rows,),
            in_specs=[pl.BlockSpec(memory_space=pl.ANY)],
            out_specs=pl.BlockSpec((rows, c), lambda i, idx: (i, 0)),
            scratch_shapes=[pltpu.VMEM((2, rows, c), src.dtype), pltpu.SemaphoreType.DMA((2,))],
        ),
        out_shape=jax.ShapeDtypeStruct((p, c), src.dtype),
        compiler_params=pltpu.CompilerParams(dimension_semantics=("arbitrary",), vmem_limit_bytes=VMEM_LIMIT,
                                             disable_bounds_checks=True),
        name="gather_rows",
    )(gidx, src)


def _ffn_up_kernel(x_ref, wg_ref, wu_ref, o_ref, xb_ref):
    @pl.when(pl.program_id(2) == 0)
    def _():
        xb_ref[...] = _unpack_halves(x_ref[...])

    x = xb_ref[...]
    a = jnp.dot(x, wg_ref[...], preferred_element_type=F32)
    b = jnp.dot(x, wu_ref[...], preferred_element_type=F32)
    o_ref[...] = (a * _sigmoid(a) * b).astype(o_ref.dtype)


def ffn_up(xe, w_gate, w_up, layer, n_exp):
    p, c = xe.shape
    d, f = w_gate.shape[-2:]
    capt = p // n_exp
    tm, tn = _tile(capt, 1024, 16), _tile(f, 512, LANES)
    nt = capt // tm
    return pl.pallas_call(
        _ffn_up_kernel,
        grid=(n_exp, nt, f // tn),
        in_specs=[
            pl.BlockSpec((tm, c), lambda e, i, j: (e * nt + i, 0)),
            pl.BlockSpec((None, None, d, tn), lambda e, i, j: (layer, e, 0, j)),
            pl.BlockSpec((None, None, d, tn), lambda e, i, j: (layer, e, 0, j)),
        ],
        out_specs=pl.BlockSpec((tm, tn), lambda e, i, j: (e * nt + i, j)),
        out_shape=jax.ShapeDtypeStruct((p, f), BF16),
        scratch_shapes=[pltpu.VMEM((tm, d), BF16)],
        compiler_params=_params("parallel", "parallel", "arbitrary"),
        name="ffn_up",
    )(xe, w_gate, w_up)


def _ffn_down_kernel(h_ref, w_ref, g_ref, o_ref):
    y = jnp.dot(h_ref[...], w_ref[...], preferred_element_type=F32) * g_ref[...]
    o_ref[...] = y.astype(o_ref.dtype)


def ffn_down(hid, w_down, gate_col, layer, n_exp):
    p, f = hid.shape
    d = w_down.shape[-1]
    capt = p // n_exp
    tm, tn = _tile(capt, 1024, 16), _tile(d, 1024, LANES)
    nt = capt // tm
    return pl.pallas_call(
        _ffn_down_kernel,
        grid=(n_exp, nt, d // tn),
        in_specs=[
            pl.BlockSpec((tm, f), lambda e, i, j: (e * nt + i, 0)),
            pl.BlockSpec((None, None, f, tn), lambda e, i, j: (layer, e, 0, j)),
            pl.BlockSpec((tm, 1), lambda e, i, j: (e * nt + i, 0)),
        ],
        out_specs=pl.BlockSpec((tm, tn), lambda e, i, j: (e * nt + i, j)),
        out_shape=jax.ShapeDtypeStruct((p, d), BF16),
        compiler_params=_params("parallel", "parallel", "parallel"),
        name="ffn_down",
    )(hid, w_down, gate_col)


SLAB = 64


def _combine_kernel(lo_ref, hi_ref, base_ref, h_ref, pos_ref, y_ref, o_ref, buf, xbuf, sem, xsem,
                    *, n_exp, bpt, p_rows, tile0):
    j = pl.program_id(1)
    ni, nj = pl.num_programs(0), pl.num_programs(1)
    i = pl.program_id(0) + tile0
    nblk = lo_ref.shape[0] // n_exp
    tt, dc = h_ref.shape
    step = pl.program_id(0) * nj + j
    slot = step % 2

    def window(tile, e):
        base = base_ref[e * nblk + tile * bpt]
        lo_al = (lo_ref[e * nblk + tile * bpt] // 16) * 16
        row = jnp.minimum(base + lo_al, p_rows - SLAB)
        return row - base, pl.multiple_of(row, 16)

    def fetch(tile, col, to_slot):
        for e in range(n_exp):
            _, row = window(tile, e)
            pltpu.make_async_copy(y_ref.at[pl.ds(row, SLAB), pl.ds(pl.multiple_of(col * dc, LANES), dc)],
                                  buf.at[to_slot, pl.ds(e * SLAB, SLAB), :], sem.at[to_slot]).start()

    @pl.when(step == 0)
    def _():
        fetch(tile0, 0, 0)

    pltpu.make_async_copy(y_ref.at[pl.ds(0, n_exp * SLAB), pl.ds(0, dc)], buf.at[slot], sem.at[slot]).wait()

    @pl.when(step + 1 < ni * nj)
    def _():
        nxt = step + 1
        fetch(tile0 + nxt // nj, nxt % nj, 1 - slot)

    jlane = lax.broadcasted_iota(I32, (tt, SLAB), 1)
    pieces = []
    for e in range(n_exp):
        first, _ = window(i, e)
        pieces.append(jnp.where(pos_ref[:, e:e + 1] - first == jlane, 1.0, 0.0))
    sel = jnp.concatenate(pieces, axis=1).astype(BF16)
    o_ref[...] = h_ref[...] + jnp.dot(sel, buf[slot], preferred_element_type=F32)

    for e in range(n_exp):
        first, _ = window(i, e)
        hi = hi_ref[e * nblk + i * bpt + bpt - 1]
        base = base_ref[e * nblk + i * bpt]
        pos = pos_ref[:, e:e + 1]

        def extra(c, carry):
            start = first + (c + 1) * SLAB
            row = jnp.minimum(base + start, p_rows - SLAB)
            cp = pltpu.make_async_copy(y_ref.at[pl.ds(pl.multiple_of(row, 16), SLAB),
                                                pl.ds(pl.multiple_of(j * dc, LANES), dc)], xbuf, xsem)
            cp.start()
            cp.wait()
            more = ((pos - (row - base) == jlane) & (pos >= start)).astype(BF16)
            o_ref[...] += jnp.dot(more, xbuf[...], preferred_element_type=F32)
            return carry

        lax.fori_loop(0, jnp.maximum(hi - first - 1, 0) // SLAB, extra, 0)


def combine(h, pos_t, y, lo_flat, hi_flat, base_flat, n_exp, tok0=0, ntok=None):
    n, d = h.shape
    ntok = n if ntok is None else ntok
    tt = _tile(n, 256, LANES)
    assert tok0 % tt == 0 and ntok % tt == 0
    tile0 = tok0 // tt
    dc = d
    return pl.pallas_call(
        functools.partial(_combine_kernel, n_exp=n_exp, bpt=tt // LANES, p_rows=y.shape[0], tile0=tile0),
        grid_spec=pltpu.PrefetchScalarGridSpec(
            num_scalar_prefetch=3,
            grid=(ntok // tt, d // dc),
            in_specs=[
                pl.BlockSpec((tt, dc), lambda i, j, *_: (i + tile0, j)),
                pl.BlockSpec((tt, LANES), lambda i, j, *_: (i + tile0, 0)),
                pl.BlockSpec(memory_space=pl.ANY),
            ],
            out_specs=pl.BlockSpec((tt, dc), lambda i, j, *_: (i, j)),
            scratch_shapes=[pltpu.VMEM((2, n_exp * SLAB, dc), BF16), pltpu.VMEM((SLAB, dc), BF16),
                            pltpu.SemaphoreType.DMA((2,)), pltpu.SemaphoreType.DMA(())],
        ),
        out_shape=jax.ShapeDtypeStruct((ntok, d), F32),
        compiler_params=pltpu.CompilerParams(dimension_semantics=("arbitrary", "arbitrary"),
                                             vmem_limit_bytes=VMEM_LIMIT, disable_bounds_checks=True),
        name="combine",
    )(lo_flat, hi_flat, base_flat, h, pos_t, y)


def expert_choice_ffn(h, norm_g, w_router_pad, w_gate, w_up, w_down, layer, seqs, cfg, split=None):
    n, d = h.shape
    ne = cfg.n_experts
    hn, aff = norm_proj(h, norm_g, w_router_pad, jnp.zeros((1, LANES), F32), eps=cfg.eps, softmax_cols=ne,
                        packed=True)
    aff_t = aff[:, :ne].T
    caps = [cfg.ec_capacity * s // ne for _, s in seqs]
    capt = sum(caps)
    idx_l, gate_l, pos_l, lo_l, hi_l, base_l = [], [], [], [], [], []
    row0 = 0
    for (off, s), cap in zip(seqs, caps):
        idx, gate, pos, lo = route(aff_t[:, off:off + s], cap)
        idx_l.append(idx + off)
        gate_l.append(gate)
        pos_l.append(pos)
        lo_l.append(lo)
        hi_l.append(jnp.concatenate([lo[:, 1:], jnp.full((ne, 1), cap, I32)], axis=1))
        base_l.append(jnp.broadcast_to((jnp.arange(ne, dtype=I32) * capt + row0)[:, None], lo.shape))
        row0 += cap
    gidx = jnp.concatenate(idx_l, axis=1).reshape(-1)
    gate_col = jnp.concatenate(gate_l, axis=1).reshape(-1, 1)
    pos_t = _pad_cols(jnp.concatenate(pos_l, axis=1).T)
    lo_flat, hi_flat, base_flat = (jnp.concatenate(t, axis=1).reshape(-1) for t in (lo_l, hi_l, base_l))

    xe = gather_rows(hn, gidx)
    hid = ffn_up(xe, w_gate, w_up, layer, ne)
    y = ffn_down(hid, w_down, gate_col, layer, ne)
    if split is None:
        return combine(h, pos_t, y, lo_flat, hi_flat, base_flat, ne)
    return [combine(h, pos_t, y, lo_flat, hi_flat, base_flat, ne, t0, nt) for t0, nt in split]


def _pad_cols(w, cols=LANES):
    return jnp.pad(w, [(0, 0)] * (w.ndim - 1) + [(0, cols - w.shape[-1])])


def trunk(x, seqs, out_split, p, cfg):
    n, d = x.shape
    nh = cfg.ml_heads
    depth = p["w_in"].shape[0]
    o_g = 3 * cfg.naw + 2 * cfg.qkw + 2 * cfg.vw
    w_in = p["w_in"]
    w_main = jnp.concatenate([w_in[:, :, :o_g], w_in[:, :, o_g + 4 * nh:]], axis=-1).astype(BF16)
    w_gates = _pad_cols(w_in[:, :, o_g:o_g + 4 * nh]).astype(BF16)
    b_gates = _pad_cols(p["b_gate"])[:, None, :]
    w_pa, w_pm, w_out = (p[k].astype(BF16) for k in ("w_pa", "w_pm", "w_out"))
    w_router = _pad_cols(p["w_router"]).astype(BF16)
    w_fg, w_fu, w_fd = (p[k].astype(BF16) for k in ("w_ff_gate", "w_ff_up", "w_ff_down"))
    na_bias = na_bias_table(p["rpb"], cfg)
    col_scale = jnp.concatenate([jnp.full((1, cfg.qkw), cfg.ml_qk ** -0.5, F32), jnp.ones((1, cfg.qkw), F32)], axis=1)

    for l in range(depth):
        xn, gates = norm_proj(x, p["norm1_g"][l][None], w_gates[l], b_gates[l], eps=cfg.eps)
        z = in_proj(xn, w_main, l)
        ya = neighborhood_attention(z, na_bias[l], p["q_gain"][l][None], p["k_gain"][l][None], seqs, cfg)
        mqk = conv_silu(z, p["conv_w"][l], p["conv_b"][l][None], col_scale, seqs, cfg)
        kt = mqk[:, cfg.qkw:].T
        g_dir = jnp.stack([_pad_cols(gates[:, 0:2 * nh]), _pad_cols(gates[:, 2 * nh:4 * nh])])
        pc = gate_cumsum(g_dir, cfg)
        pr = jnp.transpose(pc[:, :, :2 * nh], (0, 2, 1))
        hdir = mlstm(mqk, kt, z, pc, pr, seqs, cfg)
        ym = ml_out(hdir, z, p["ml_gain"][l].reshape(1, -1), cfg)
        merged = merge_proj(ya, ym, w_pa, w_pm, z, l, cfg)
        h = out_proj(merged, w_out, x, l)
        x = expert_choice_ffn(h, p["norm2_g"][l][None], w_router[l], w_fg, w_fu, w_fd, l, seqs, cfg,
                              split=out_split if l == depth - 1 else None)
    return x


def kernel(x_prompt, x_sample, norm1_g, norm2_g, w_in, b_gate, conv_w, conv_b, q_gain, k_gain, rpb, ml_gain,
           w_pa, w_pm, w_out, w_router, w_ff_gate, w_ff_up, w_ff_down):
    cfg = CFG
    d = x_prompt.shape[-1]
    xs = [x_prompt.reshape(-1, d), x_sample.reshape(-1, d)]
    seqs, off = [], 0
    for xx in (x_prompt, x_sample):
        for _ in range(xx.shape[0]):
            seqs.append((off, xx.shape[1]))
            off += xx.shape[1]
    p = dict(norm1_g=norm1_g, norm2_g=norm2_g, w_in=w_in, b_gate=b_gate, conv_w=conv_w, conv_b=conv_b,
             q_gain=q_gain, k_gain=k_gain, rpb=rpb, ml_gain=ml_gain, w_pa=w_pa, w_pm=w_pm, w_out=w_out,
             w_router=w_router, w_ff_gate=w_ff_gate, w_ff_up=w_ff_up, w_ff_down=w_ff_down)
    n_p, n_s = xs[0].shape[0], xs[1].shape[0]
    y_p, y_s = trunk(jnp.concatenate(xs, axis=0), seqs, [(0, n_p), (n_p, n_s)], p, cfg)
    return (y_p.reshape(x_prompt.shape), y_s.reshape(x_sample.shape))
```

```python
import dataclasses
import functools

import jax
import jax.numpy as jnp
import numpy as np
from jax import lax
from jax.experimental import pallas as pl
from jax.experimental.pallas import tpu as pltpu

F32 = jnp.float32
BF16 = jnp.bfloat16
I32 = jnp.int32
NEG = -1e30
LANES = 128
VMEM_LIMIT = 56 * 1024 * 1024


@dataclasses.dataclass(frozen=True)
class Cfg:
    d_model: int = 4096
    grid_w: int = 64
    na_heads: int = 16
    na_dim: int = 128
    na_rows: int = 8
    na_cols: int = 16
    ml_heads: int = 8
    ml_qk: int = 128
    ml_v: int = 256
    ml_conv: int = 5
    n_experts: int = 16
    ec_capacity: int = 2
    d_ff: int = 2048
    eps: float = 1e-6
    ml_chunk: int = 256

    @property
    def naw(self):
        return self.na_heads * self.na_dim

    @property
    def qkw(self):
        return self.ml_heads * self.ml_qk

    @property
    def vw(self):
        return self.ml_heads * self.ml_v

    @property
    def z_na_k(self):
        return self.naw

    @property
    def z_na_v(self):
        return 2 * self.naw

    @property
    def z_ml_q(self):
        return 3 * self.naw

    @property
    def z_ml_v(self):
        return self.z_ml_q + 2 * self.qkw

    @property
    def z_ml_o(self):
        return self.z_ml_v + self.vw

    @property
    def z_merge(self):
        return self.z_ml_o + self.vw

    @property
    def z_cols(self):
        return self.z_merge + 2 * self.d_model


CFG = Cfg()


def _tile(dim, pref, mult=8):
    if dim <= pref:
        return dim
    t = (pref // mult) * mult
    while t > mult and dim % t:
        t -= mult
    assert dim % t == 0, (dim, pref, mult)
    return t


def _params(*sem):
    return pltpu.CompilerParams(dimension_semantics=sem, vmem_limit_bytes=VMEM_LIMIT)


def _sigmoid(x):
    return 1.0 / (1.0 + jnp.exp(-x))


def _pack_halves(x):
    half = x.shape[1] // 2
    lo = lax.bitcast_convert_type(x[:, :half].astype(F32), jnp.uint32)
    hi = lax.bitcast_convert_type(x[:, half:].astype(F32), jnp.uint32)
    return (lo >> 16) | (hi & jnp.uint32(0xFFFF0000))


def _unpack_halves(w):
    lo = lax.bitcast_convert_type(w << 16, F32).astype(BF16)
    hi = lax.bitcast_convert_type(w & jnp.uint32(0xFFFF0000), F32).astype(BF16)
    return jnp.concatenate([lo, hi], axis=1)


def _norm_proj_kernel(x_ref, g_ref, w_ref, b_ref, xn_ref, s_ref, *, eps, softmax_cols, packed):
    x = x_ref[...]
    ms = jnp.mean(x * x, axis=-1, keepdims=True)
    xn = (x * lax.rsqrt(ms + eps) * g_ref[...]).astype(BF16)
    xn_ref[...] = _pack_halves(xn) if packed else xn
    s = jnp.dot(xn, w_ref[...], preferred_element_type=F32) + b_ref[...]
    if softmax_cols:
        lane = lax.broadcasted_iota(I32, s.shape, 1)
        s = jnp.where(lane < softmax_cols, s, NEG)
        e = jnp.exp(s - jnp.max(s, axis=-1, keepdims=True))
        s = e / jnp.sum(e, axis=-1, keepdims=True)
    s_ref[...] = s


def norm_proj(x, g, w_small, b_small, *, eps, softmax_cols=0, packed=False):
    n, d = x.shape
    tm = _tile(n, 512)
    xn_t = jax.eval_shape(_pack_halves, jax.ShapeDtypeStruct((n, d), BF16)) if packed else jax.ShapeDtypeStruct((n, d), BF16)
    return pl.pallas_call(
        functools.partial(_norm_proj_kernel, eps=eps, softmax_cols=softmax_cols, packed=packed),
        grid=(n // tm,),
        in_specs=[
            pl.BlockSpec((tm, d), lambda i: (i, 0)),
            pl.BlockSpec((1, d), lambda i: (0, 0)),
            pl.BlockSpec((d, LANES), lambda i: (0, 0)),
            pl.BlockSpec((1, LANES), lambda i: (0, 0)),
        ],
        out_specs=[
            pl.BlockSpec((tm, xn_t.shape[1]), lambda i: (i, 0)),
            pl.BlockSpec((tm, LANES), lambda i: (i, 0)),
        ],
        out_shape=[xn_t, jax.ShapeDtypeStruct((n, LANES), F32)],
        compiler_params=_params("parallel"),
        name="norm_proj",
    )(x, g, w_small, b_small)


def _mm_kernel(x_ref, w_ref, o_ref):
    o_ref[...] = jnp.dot(x_ref[...], w_ref[...], preferred_element_type=F32).astype(o_ref.dtype)


def in_proj(xn, w_main, layer, tm_pref=1024, tn_pref=1024, x_buffers=2):
    n, d = xn.shape
    zc = w_main.shape[-1]
    tm, tn = _tile(n, tm_pref), _tile(zc, tn_pref, LANES)
    return pl.pallas_call(
        _mm_kernel,
        grid=(n // tm, zc // tn),
        in_specs=[
            pl.BlockSpec((tm, d), lambda i, j: (i, 0), pipeline_mode=pl.Buffered(x_buffers)),
            pl.BlockSpec((None, d, tn), lambda i, j: (layer, 0, j)),
        ],
        out_specs=pl.BlockSpec((tm, tn), lambda i, j: (i, j)),
        out_shape=jax.ShapeDtypeStruct((n, zc), BF16),
        compiler_params=_params("parallel", "parallel"),
        name="in_proj",
    )(xn, w_main)


def _merge_kernel(ya_ref, ym_ref, wa_ref, wm_ref, ga_ref, gm_ref, o_ref):
    a = jnp.dot(ya_ref[...], wa_ref[...], preferred_element_type=F32)
    m = jnp.dot(ym_ref[...], wm_ref[...], preferred_element_type=F32)
    o = _sigmoid(ga_ref[...].astype(F32)) * a + _sigmoid(gm_ref[...].astype(F32)) * m
    o_ref[...] = o.astype(o_ref.dtype)


def merge_proj(ya, ym, w_pa, w_pm, z, layer, cfg):
    n = ya.shape[0]
    d = cfg.d_model
    tm, tn = _tile(n, 1024), _tile(d, 1024, LANES)
    assert cfg.z_merge % tn == 0
    ga0 = cfg.z_merge // tn
    gm0 = (cfg.z_merge + d) // tn
    return pl.pallas_call(
        _merge_kernel,
        grid=(n // tm, d // tn),
        in_specs=[
            pl.BlockSpec((tm, cfg.naw), lambda i, j: (i, 0)),
            pl.BlockSpec((tm, cfg.vw), lambda i, j: (i, 0)),
            pl.BlockSpec((None, cfg.naw, tn), lambda i, j: (layer, 0, j)),
            pl.BlockSpec((None, cfg.vw, tn), lambda i, j: (layer, 0, j)),
            pl.BlockSpec((tm, tn), lambda i, j: (i, ga0 + j)),
            pl.BlockSpec((tm, tn), lambda i, j: (i, gm0 + j)),
        ],
        out_specs=pl.BlockSpec((tm, tn), lambda i, j: (i, j)),
        out_shape=jax.ShapeDtypeStruct((n, d), BF16),
        compiler_params=_params("parallel", "parallel"),
        name="merge_proj",
    )(ya, ym, w_pa, w_pm, z, z)


def _out_proj_kernel(m_ref, w_ref, r_ref, o_ref):
    o_ref[...] = r_ref[...] + jnp.dot(m_ref[...], w_ref[...], preferred_element_type=F32)


def out_proj(merged, w_out, res, layer):
    n, d = merged.shape
    tm, tn = _tile(n, 1024), _tile(d, 1024, LANES)
    return pl.pallas_call(
        _out_proj_kernel,
        grid=(n // tm, d // tn),
        in_specs=[
            pl.BlockSpec((tm, d), lambda i, j: (i, 0)),
            pl.BlockSpec((None, d, tn), lambda i, j: (layer, 0, j)),
            pl.BlockSpec((tm, tn), lambda i, j: (i, j)),
        ],
        out_specs=pl.BlockSpec((tm, tn), lambda i, j: (i, j)),
        out_shape=jax.ShapeDtypeStruct((n, d), F32),
        compiler_params=_params("parallel", "parallel"),
        name="out_proj",
    )(merged, w_out, res)


NA_GROUP = 4
NA_CHUNK_ROWS = 128
NA_UNROLL = 32


def na_bias_table(rpb, cfg):
    w_, kr, nc, gr = cfg.grid_w, cfg.na_rows, cfg.na_cols, NA_GROUP
    ur = kr + gr
    assert gr <= kr // 2
    cols = np.arange(w_)
    col0 = np.clip(cols - nc // 2, 0, w_ - nc)
    valid_c = (cols[None, :] >= col0[:, None]) & (cols[None, :] < col0[:, None] + nc)
    dc = cols[None, :] - cols[:, None] + (nc - 1)
    cm = (np.arange(2 * nc - 1)[:, None, None] == dc[None]) & valid_c[None]
    j = np.arange(gr)
    off = np.stack([0 * j, j, 0 * j + (ur - kr)])
    delta = np.stack([j, 0 * j + kr // 2, kr - gr + j])
    k = np.arange(ur)[None, None, :] - off[:, :, None]
    valid_r = (k >= 0) & (k < kr)
    dr = k - delta[:, :, None] + (kr - 1)
    rm = (np.arange(2 * kr - 1)[:, None, None, None] == dr[None]) & valid_r[None]
    t = jnp.einsum("lhab,atju,bwc->lhtjwuc", rpb.astype(F32), rm.astype(np.float32), cm.astype(np.float32),
                   precision=lax.Precision.HIGHEST)
    valid = valid_r[:, :, None, :, None] & valid_c[None, None, :, None, :]
    t = jnp.where(valid[None, None], t, NEG)
    return t.reshape(rpb.shape[0], rpb.shape[1], 3, gr * w_, ur * w_)


def _na_kernel(q_ref, k_ref, v_ref, b_ref, qg_ref, kg_ref, o_ref, kn_ref, vx_ref, *, row_bounds, chunk_rows, cfg):
    w_, kr, gr = cfg.grid_w, cfg.na_rows, NA_GROUP
    band = kr * w_
    gq = gr * w_
    un = (kr + gr) * w_
    eps = cfg.eps
    c = pl.program_id(1)
    n_rows = k_ref.shape[0] // w_

    @pl.when(c == 0)
    def _():
        kg = kg_ref[...]

        ones_col = jnp.where(lax.broadcasted_iota(I32, (band, LANES), 1) == 0, 1.0, 0.0).astype(BF16)

        def knorm(i, carry):
            sl = pl.ds(pl.multiple_of(i * band, band), band)
            kk = k_ref[sl, :].astype(F32)
            ms = jnp.mean(kk * kk, axis=-1, keepdims=True)
            kn_ref[sl, :] = (kk * lax.rsqrt(ms + eps) * kg).astype(BF16)
            vx_ref[sl, :] = jnp.concatenate([v_ref[sl, :], ones_col], axis=1)
            return carry

        lax.fori_loop(0, n_rows // kr, knorm, 0)

    qg = qg_ref[...] * (cfg.na_dim ** -0.5)
    g0 = c * chunk_rows
    lo = functools.reduce(lambda acc, b: jnp.where(g0 >= b[0], b[0], acc), row_bounds, 0)
    hi = functools.reduce(lambda acc, b: jnp.where(g0 >= b[0], b[1], acc), row_bounds, 0)

    def group(i, carry):
        first = g0 + i * gr
        u0 = jnp.clip(first - kr // 2, lo, hi - (kr + gr))
        kind = jnp.where(first == lo, 0, jnp.where(first == hi - gr, 2, 1))
        qsl = pl.ds(pl.multiple_of(i * gq, gq), gq)
        usl = pl.ds(pl.multiple_of(u0 * w_, w_), un)
        q = q_ref[qsl, :].astype(F32)
        ms = jnp.mean(q * q, axis=-1, keepdims=True)
        qn = (q * lax.rsqrt(ms + eps) * qg).astype(BF16)
        s = lax.dot_general(qn, kn_ref[usl, :], (((1,), (1,)), ((), ())), preferred_element_type=F32)
        s = s + b_ref[0, kind]
        e = jnp.exp((s - jnp.max(s, axis=-1, keepdims=True)).astype(BF16))
        o = jnp.dot(e, vx_ref[usl, :], preferred_element_type=F32)
        o_ref[qsl, :] = (o[:, :LANES] / o[:, LANES:LANES + 1]).astype(o_ref.dtype)
        return carry

    lax.fori_loop(0, chunk_rows // gr, group, 0, unroll=min(NA_UNROLL, chunk_rows // gr))


def neighborhood_attention(z, bias, q_gain, k_gain, seqs, cfg):
    n = z.shape[0]
    hd = cfg.na_dim
    w_ = cfg.grid_w
    chunk_rows = functools.reduce(np.gcd, [s // w_ for _, s in seqs] + [o // w_ for o, _ in seqs] + [NA_CHUNK_ROWS])
    cq = int(chunk_rows) * w_
    gr, ur = NA_GROUP, cfg.na_rows + NA_GROUP
    assert hd == LANES and all(o % cq == 0 and s % cq == 0 and s // w_ >= ur for o, s in seqs)
    assert chunk_rows % gr == 0 and (n // w_) % cfg.na_rows == 0
    row_bounds = [(o // w_, (o + s) // w_) for o, s in seqs]
    kq, kk, kv = 0, cfg.z_na_k // hd, cfg.z_na_v // hd
    return pl.pallas_call(
        functools.partial(_na_kernel, row_bounds=row_bounds, chunk_rows=int(chunk_rows), cfg=cfg),
        grid=(cfg.na_heads, n // cq),
        in_specs=[
            pl.BlockSpec((cq, hd), lambda h, c: (c, kq + h)),
            pl.BlockSpec((n, hd), lambda h, c: (0, kk + h), pipeline_mode=pl.Buffered(1)),
            pl.BlockSpec((n, hd), lambda h, c: (0, kv + h), pipeline_mode=pl.Buffered(1)),
            pl.BlockSpec((1, 3, gr * w_, ur * w_), lambda h, c: (h, 0, 0, 0)),
            pl.BlockSpec((1, hd), lambda h, c: (0, 0)),
            pl.BlockSpec((1, hd), lambda h, c: (0, 0)),
        ],
        out_specs=pl.BlockSpec((cq, hd), lambda h, c: (c, h)),
        out_shape=jax.ShapeDtypeStruct((n, cfg.naw), BF16),
        scratch_shapes=[pltpu.VMEM((n, hd), BF16), pltpu.VMEM((n, 2 * hd), BF16)],
        compiler_params=_params("arbitrary", "arbitrary"),
        name="neighborhood_attention",
    )(z, z, z, bias, q_gain, k_gain)


HALO = 16


def _conv_kernel(x_ref, p_ref, n_ref, w_ref, b_ref, s_ref, o_ref, *, ts, taps, starts, ends):
    i = pl.program_id(1)
    is_start = functools.reduce(jnp.logical_or, [i == s // ts for s in starts])
    is_end = functools.reduce(jnp.logical_or, [i == e // ts - 1 for e in ends])
    xp = p_ref[...].astype(F32) * jnp.where(is_start, 0.0, 1.0)
    xn = n_ref[...].astype(F32) * jnp.where(is_end, 0.0, 1.0)
    ext = jnp.concatenate([xp, x_ref[...].astype(F32), xn], axis=0)
    w = w_ref[...]
    half = taps // 2
    acc = jnp.zeros((ts, x_ref.shape[1]), F32) + b_ref[...]
    for j in range(taps):
        lo = HALO + j - half
        acc = acc + w[j:j + 1, :] * ext[lo:lo + ts, :]
    o_ref[...] = (acc * _sigmoid(acc) * s_ref[...]).astype(o_ref.dtype)


def conv_silu(z, conv_w, conv_b, col_scale, seqs, cfg):
    n = z.shape[0]
    c = 2 * cfg.qkw
    cb = _tile(c, 512, LANES)
    ts = _tile(min(s for _, s in seqs), 1024, HALO)
    assert all(o % ts == 0 and s % ts == 0 for o, s in seqs) and cfg.z_ml_q % cb == 0
    c0 = cfg.z_ml_q // cb
    nh = n // HALO
    starts = [o for o, _ in seqs]
    ends = [o + s for o, s in seqs]
    return pl.pallas_call(
        functools.partial(_conv_kernel, ts=ts, taps=cfg.ml_conv, starts=starts, ends=ends),
        grid=(c // cb, n // ts),
        in_specs=[
            pl.BlockSpec((ts, cb), lambda j, i: (i, c0 + j)),
            pl.BlockSpec((HALO, cb), lambda j, i: (jnp.maximum(i * (ts // HALO) - 1, 0), c0 + j)),
            pl.BlockSpec((HALO, cb), lambda j, i: (jnp.minimum((i + 1) * (ts // HALO), nh - 1), c0 + j)),
            pl.BlockSpec((cfg.ml_conv, cb), lambda j, i: (0, j)),
            pl.BlockSpec((1, cb), lambda j, i: (0, j)),
            pl.BlockSpec((1, cb), lambda j, i: (0, j)),
        ],
        out_specs=pl.BlockSpec((ts, cb), lambda j, i: (i, j)),
        out_shape=jax.ShapeDtypeStruct((n, c), BF16),
        compiler_params=_params("parallel", "parallel"),
        name="conv_silu",
    )(z, z, z, conv_w, conv_b, col_scale)


def _split3(x):
    hi = x.astype(BF16)
    r = x - hi.astype(F32)
    mid = r.astype(BF16)
    lo = (r - mid.astype(F32)).astype(BF16)
    return hi, mid, lo


def _gate_kernel(g_ref, o_ref, *, nh, ell):
    d = pl.program_id(0)
    t = lax.broadcasted_iota(I32, (ell, ell), 0)
    u = lax.broadcasted_iota(I32, (ell, ell), 1)
    tri = jnp.where((u - t) * jnp.where(d == 0, 1, -1) <= 0, 1.0, 0.0).astype(BF16)
    lane = lax.broadcasted_iota(I32, (ell, LANES), 1)
    for k in range(g_ref.shape[0] // ell):
        sl = slice(k * ell, (k + 1) * ell)
        g = g_ref[sl, :]
        lf = jnp.minimum(g, 0.0) - jnp.log(1.0 + jnp.exp(-jnp.abs(g)))
        cs = sum(jnp.dot(tri, p, preferred_element_type=F32) for p in _split3(lf))
        o_ref[sl, :] = jnp.where((lane >= nh) & (lane < 2 * nh), cs, g)


def gate_cumsum(g_dir, cfg):
    _, n, _ = g_dir.shape
    ell = cfg.ml_chunk
    rows = _tile(n, 8 * ell, ell)
    return pl.pallas_call(
        functools.partial(_gate_kernel, nh=cfg.ml_heads, ell=ell),
        grid=(2, n // rows),
        in_specs=[pl.BlockSpec((None, rows, LANES), lambda d, c: (d, c, 0))],
        out_specs=pl.BlockSpec((None, rows, LANES), lambda d, c: (d, c, 0)),
        out_shape=jax.ShapeDtypeStruct(g_dir.shape, F32),
        compiler_params=_params("parallel", "parallel"),
        name="gate_cumsum",
    )(g_dir)


def _mlstm_kernel(q_ref, kt_ref, v_ref, pc_ref, pr_ref, o_ref, c_ref, m_ref, *, nc, firsts, lasts, cfg):
    d = pl.program_id(0)
    c = pl.program_id(1)
    ci = jnp.where(d == 0, c, nc - 1 - c)
    nh, dk, dv, ell = cfg.ml_heads, cfg.ml_qk, cfg.ml_v, cfg.ml_chunk
    dve = dv + LANES

    fwd_reset = functools.reduce(jnp.logical_or, [ci == f for f in firsts])
    bwd_reset = functools.reduce(jnp.logical_or, [ci == f for f in lasts])

    @pl.when(jnp.where(d == 0, fwd_reset, bwd_reset))
    def _():
        c_ref[...] = jnp.zeros_like(c_ref)
        m_ref[...] = jnp.zeros_like(m_ref)

    t = lax.broadcasted_iota(I32, (ell, ell), 0)
    s = lax.broadcasted_iota(I32, (ell, ell), 1)
    tri = (s - t) * jnp.where(d == 0, 1, -1) <= 0
    ones_col = jnp.where(lax.broadcasted_iota(I32, (ell, LANES), 1) == 0, 1.0, 0.0).astype(BF16)
    pc = pc_ref[...]
    pr = pr_ref[...]

    for h in range(nh):
        qh = q_ref[:, h * dk:(h + 1) * dk]
        kth = kt_ref[h * dk:(h + 1) * dk, :]
        vext = jnp.concatenate([v_ref[:, h * dv:(h + 1) * dv], ones_col], axis=1)
        bcol = pc[:, nh + h:nh + h + 1]
        irow = pr[h:h + 1, :]
        brow = pr[nh + h:nh + h + 1, :]
        tot = jnp.where(d == 0, brow[:, ell - 1:ell], brow[:, 0:1])
        m_prev = m_ref[h][0:1, 0:1]
        cext = c_ref[h]

        dm = jnp.where(tri, bcol - (brow - irow), NEG)
        inter = bcol + m_prev
        m_t = jnp.maximum(inter, jnp.max(dm, axis=1, keepdims=True))
        sc = jnp.exp(inter - m_t)
        a = jnp.dot(qh, kth, preferred_element_type=F32) * jnp.exp(dm - m_t)
        nd = jnp.dot(a.astype(BF16), vext, preferred_element_type=F32)
        nd = nd + sc * jnp.dot(qh, cext.astype(BF16), preferred_element_type=F32)
        den = jnp.maximum(jnp.abs(nd[:, dv:dv + 1]), jnp.exp(-m_t))
        o_ref[:, h * dv:(h + 1) * dv] = nd[:, :dv] / den

        dend = tot - brow + irow
        m_new = jnp.maximum(tot + m_prev, jnp.max(dend, axis=1, keepdims=True))
        wk = jnp.exp(dend - m_new)
        decay = jnp.exp(tot + m_prev - m_new)
        kw = (kth.astype(F32) * wk).astype(BF16)
        c_ref[h] = decay * cext + jnp.dot(kw, vext, preferred_element_type=F32)
        m_ref[h] = jnp.broadcast_to(m_new, m_ref.shape[1:])


def mlstm(mqk, kt, z, pc, pr, seqs, cfg):
    n = z.shape[0]
    ell = cfg.ml_chunk
    nc = n // ell
    assert all(o % ell == 0 and s % ell == 0 for o, s in seqs)
    firsts = [o // ell for o, _ in seqs]
    lasts = [(o + s) // ell - 1 for o, s in seqs]
    qkw, vw = cfg.qkw, cfg.vw
    assert cfg.z_ml_v % vw == 0
    v0 = cfg.z_ml_v // vw

    def cidx(d, c):
        return jnp.where(d == 0, c, nc - 1 - c)

    return pl.pallas_call(
        functools.partial(_mlstm_kernel, nc=nc, firsts=firsts, lasts=lasts, cfg=cfg),
        grid=(2, nc),
        in_specs=[
            pl.BlockSpec((ell, qkw), lambda d, c: (cidx(d, c), 0)),
            pl.BlockSpec((qkw, ell), lambda d, c: (0, cidx(d, c))),
            pl.BlockSpec((ell, vw), lambda d, c: (cidx(d, c), v0)),
            pl.BlockSpec((None, ell, LANES), lambda d, c: (d, cidx(d, c), 0)),
            pl.BlockSpec((None, 2 * cfg.ml_heads, ell), lambda d, c: (d, 0, cidx(d, c))),
        ],
        out_specs=pl.BlockSpec((None, ell, vw), lambda d, c: (d, cidx(d, c), 0)),
        out_shape=jax.ShapeDtypeStruct((2, n, vw), F32),
        scratch_shapes=[
            pltpu.VMEM((cfg.ml_heads, cfg.ml_qk, cfg.ml_v + LANES), F32),
            pltpu.VMEM((cfg.ml_heads, 8, LANES), F32),
        ],
        compiler_params=_params("arbitrary", "arbitrary"),
        name="mlstm",
    )(mqk, kt, z, pc, pr)


def _ml_out_kernel(hf_ref, hb_ref, og_ref, g_ref, o_ref, *, nh, dv, eps):
    for h in range(nh):
        sl = slice(h * dv, (h + 1) * dv)
        x = hf_ref[:, sl] + hb_ref[:, sl]
        ms = jnp.mean(x * x, axis=-1, keepdims=True)
        y = x * lax.rsqrt(ms + eps) * g_ref[:, sl]
        o_ref[:, sl] = (y * _sigmoid(og_ref[:, sl].astype(F32))).astype(o_ref.dtype)


def ml_out(hdir, z, ml_gain, cfg):
    _, n, vw = hdir.shape
    tm = _tile(n, 512)
    assert cfg.z_ml_o % vw == 0
    o0 = cfg.z_ml_o // vw
    return pl.pallas_call(
        functools.partial(_ml_out_kernel, nh=cfg.ml_heads, dv=cfg.ml_v, eps=cfg.eps),
        grid=(n // tm,),
        in_specs=[
            pl.BlockSpec((None, tm, vw), lambda i: (0, i, 0)),
            pl.BlockSpec((None, tm, vw), lambda i: (1, i, 0)),
            pl.BlockSpec((tm, vw), lambda i: (i, o0)),
            pl.BlockSpec((1, vw), lambda i: (0, 0)),
        ],
        out_specs=pl.BlockSpec((tm, vw), lambda i: (i, 0)),
        out_shape=jax.ShapeDtypeStruct((n, vw), BF16),
        compiler_params=_params("parallel"),
        name="ml_out",
    )(hdir, hdir, z, ml_gain)


def _prefix_counts(mask_f, upper, strict_lower):
    m16 = mask_f.astype(BF16)
    local = jnp.dot(m16, upper, preferred_element_type=F32)
    rowtot = jnp.broadcast_to(local[:, LANES - 1:LANES], local.shape)
    offs = jnp.dot(strict_lower, rowtot.astype(BF16), preferred_element_type=F32)
    return local + offs, offs


def _route_kernel(a_ref, idx_ref, gate_ref, pos_ref, lo_ref, *, cap):
    a = a_ref[0]
    nb = a.shape[0]
    bits = lax.bitcast_convert_type(a, I32)

    thr = jnp.zeros((1, 1), I32)
    for bit in range(30, -1, -1):
        cand = thr | (1 << bit)
        cnt = jnp.sum(jnp.where(bits >= cand, 1.0, 0.0), keepdims=True)
        thr = jnp.where(cnt >= cap, cand, thr)

    li = lax.broadcasted_iota(I32, (LANES, LANES), 0)
    lj = lax.broadcasted_iota(I32, (LANES, LANES), 1)
    upper = jnp.where(li <= lj, 1.0, 0.0).astype(BF16)
    bi = lax.broadcasted_iota(I32, (nb, nb), 0)
    bj = lax.broadcasted_iota(I32, (nb, nb), 1)
    strict_lower = jnp.where(bj < bi, 1.0, 0.0).astype(BF16)

    gt = bits > thr
    eq = bits == thr
    need = cap - jnp.sum(jnp.where(gt, 1.0, 0.0), keepdims=True)
    eq_f = jnp.where(eq, 1.0, 0.0)
    eq_incl, _ = _prefix_counts(eq_f, upper, strict_lower)
    sel = gt | (eq & (eq_incl <= need))
    sel_f = jnp.where(sel, 1.0, 0.0)
    incl, offs = _prefix_counts(sel_f, upper, strict_lower)

    pos_ref[0] = jnp.where(sel, incl - 1.0, -1.0).astype(I32)
    lo_ref[0] = offs.astype(I32)

    p_col = lax.broadcasted_iota(I32, (cap, 1), 0).astype(F32)
    ones8 = jnp.ones((8, LANES), BF16)
    cnt_row = lax.dot_general(ones8, sel_f.astype(BF16), (((1,), (1,)), ((), ())),
                              preferred_element_type=F32)[0:1, :]
    ui = lax.broadcasted_iota(I32, (nb, nb), 0)
    uj = lax.broadcasted_iota(I32, (nb, nb), 1)
    upper_nb = jnp.where(ui <= uj, 1.0, 0.0).astype(BF16)
    cum8 = jnp.dot(jnp.broadcast_to(cnt_row, (8, nb)).astype(BF16), upper_nb, preferred_element_type=F32)
    blockcum = cum8[0:1, :]
    before = blockcum <= p_col
    jblk = jnp.sum(jnp.where(before, 1.0, 0.0), axis=1, keepdims=True)
    base = jnp.sum(jnp.where(before, cnt_row, 0.0), axis=1, keepdims=True)
    onehot = (lax.broadcasted_iota(I32, (cap, nb), 1).astype(F32) == jblk).astype(BF16)
    local_incl = incl - offs
    a_hi, a_mid, a_lo = _split3(a)
    table = jnp.concatenate([local_incl.astype(BF16), a_hi, a_mid, a_lo], axis=1)
    rows = jnp.dot(onehot, table, preferred_element_type=F32)
    rank = p_col - base + 1.0
    lane_cnt = jnp.sum(jnp.where(rows[:, :LANES] < rank, 1.0, 0.0), axis=1, keepdims=True)
    idx_ref[0] = (jblk * LANES + lane_cnt).astype(I32)
    arow = rows[:, LANES:2 * LANES] + rows[:, 2 * LANES:3 * LANES] + rows[:, 3 * LANES:]
    lane = lax.broadcasted_iota(I32, (cap, LANES), 1).astype(F32)
    gate_ref[0] = jnp.sum(jnp.where(lane == lane_cnt, arow, 0.0), axis=1, keepdims=True)


def route(aff_t, cap):
    e, s = aff_t.shape
    nb = s // LANES
    a3 = aff_t.reshape(e, nb, LANES)
    idx, gate, pos, lo = pl.pallas_call(
        functools.partial(_route_kernel, cap=cap),
        grid=(e,),
        in_specs=[pl.BlockSpec((1, nb, LANES), lambda i: (i, 0, 0))],
        out_specs=[
            pl.BlockSpec((1, cap, 1), lambda i: (i, 0, 0)),
            pl.BlockSpec((1, cap, 1), lambda i: (i, 0, 0)),
            pl.BlockSpec((1, nb, LANES), lambda i: (i, 0, 0)),
            pl.BlockSpec((1, nb, LANES), lambda i: (i, 0, 0)),
        ],
        out_shape=[
            jax.ShapeDtypeStruct((e, cap, 1), I32),
            jax.ShapeDtypeStruct((e, cap, 1), F32),
            jax.ShapeDtypeStruct((e, nb, LANES), I32),
            jax.ShapeDtypeStruct((e, nb, LANES), I32),
        ],
        compiler_params=_params("parallel"),
        name="route",
    )(a3)
    return idx[:, :, 0], gate[:, :, 0], pos.reshape(e, s), lo[:, :, 0]


def _gather_kernel(idx_ref, src_ref, o_ref, buf, sem, *, rows):
    i = pl.program_id(0)
    slot = i % 2

    def fetch(step, to_slot):
        def issue(r, carry):
            pltpu.make_async_copy(src_ref.at[pl.ds(idx_ref[step * rows + r], 1), :],
                                  buf.at[to_slot, pl.ds(r, 1), :], sem.at[to_slot]).start()
            return carry

        lax.fori_loop(0, rows, issue, 0, unroll=8)

    @pl.when(i == 0)
    def _():
        fetch(0, 0)

    @pl.when(i + 1 < pl.num_programs(0))
    def _():
        fetch(i + 1, 1 - slot)

    pltpu.make_async_copy(src_ref.at[pl.ds(0, rows), :], buf.at[slot], sem.at[slot]).wait()
    o_ref[...] = buf[slot]


def gather_rows(src, gidx):
    p = gidx.shape[0]
    c = src.shape[1]
    rows = _tile(p, 256)
    return pl.pallas_call(
        functools.partial(_gather_kernel, rows=rows),
        grid_spec=pltpu.PrefetchScalarGridSpec(
            num_scalar_prefetch=1,
            grid=(p // rows,),
            in_specs=[pl.BlockSpec(memory_space=pl.ANY)],
            out_specs=pl.BlockSpec((rows, c), lambda i, idx: (i, 0)),
            scratch_shapes=[pltpu.VMEM((2, rows, c), src.dtype), pltpu.SemaphoreType.DMA((2,))],
        ),
        out_shape=jax.ShapeDtypeStruct((p, c), src.dtype),
        compiler_params=pltpu.CompilerParams(dimension_semantics=("arbitrary",), vmem_limit_bytes=VMEM_LIMIT,
                                             disable_bounds_checks=True),
        name="gather_rows",
    )(gidx, src)


def _ffn_up_kernel(x_ref, wg_ref, wu_ref, o_ref, xb_ref):
    @pl.when(pl.program_id(2) == 0)
    def _():
        xb_ref[...] = _unpack_halves(x_ref[...])

    x = xb_ref[...]
    a = jnp.dot(x, wg_ref[...], preferred_element_type=F32)
    b = jnp.dot(x, wu_ref[...], preferred_element_type=F32)
    o_ref[...] = (a * _sigmoid(a) * b).astype(o_ref.dtype)


def ffn_up(xe, w_gate, w_up, layer, n_exp):
    p, c = xe.shape
    d, f = w_gate.shape[-2:]
    capt = p // n_exp
    tm, tn = _tile(capt, 1024, 16), _tile(f, 512, LANES)
    nt = capt // tm
    return pl.pallas_call(
        _ffn_up_kernel,
        grid=(n_exp, nt, f // tn),
        in_specs=[
            pl.BlockSpec((tm, c), lambda e, i, j: (e * nt + i, 0)),
            pl.BlockSpec((None, None, d, tn), lambda e, i, j: (layer, e, 0, j)),
            pl.BlockSpec((None, None, d, tn), lambda e, i, j: (layer, e, 0, j)),
        ],
        out_specs=pl.BlockSpec((tm, tn), lambda e, i, j: (e * nt + i, j)),
        out_shape=jax.ShapeDtypeStruct((p, f), BF16),
        scratch_shapes=[pltpu.VMEM((tm, d), BF16)],
        compiler_params=_params("parallel", "parallel", "arbitrary"),
        name="ffn_up",
    )(xe, w_gate, w_up)


def _ffn_down_kernel(h_ref, w_ref, g_ref, o_ref):
    y = jnp.dot(h_ref[...], w_ref[...], preferred_element_type=F32) * g_ref[...]
    o_ref[...] = y.astype(o_ref.dtype)


def ffn_down(hid, w_down, gate_col, layer, n_exp):
    p, f = hid.shape
    d = w_down.shape[-1]
    capt = p // n_exp
    tm, tn = _tile(capt, 1024, 16), _tile(d, 1024, LANES)
    nt = capt // tm
    return pl.pallas_call(
        _ffn_down_kernel,
        grid=(n_exp, nt, d // tn),
        in_specs=[
            pl.BlockSpec((tm, f), lambda e, i, j: (e * nt + i, 0)),
            pl.BlockSpec((None, None, f, tn), lambda e, i, j: (layer, e, 0, j)),
            pl.BlockSpec((tm, 1), lambda e, i, j: (e * nt + i, 0)),
        ],
        out_specs=pl.BlockSpec((tm, tn), lambda e, i, j: (e * nt + i, j)),
        out_shape=jax.ShapeDtypeStruct((p, d), BF16),
        compiler_params=_params("parallel", "parallel", "parallel"),
        name="ffn_down",
    )(hid, w_down, gate_col)


SLAB = 64


def _combine_kernel(lo_ref, hi_ref, base_ref, h_ref, pos_ref, y_ref, o_ref, buf, xbuf, sem, xsem,
                    *, n_exp, bpt, p_rows, tile0):
    j = pl.program_id(1)
    ni, nj = pl.num_programs(0), pl.num_programs(1)
    i = pl.program_id(0) + tile0
    nblk = lo_ref.shape[0] // n_exp
    tt, dc = h_ref.shape
    step = pl.program_id(0) * nj + j
    slot = step % 2

    def window(tile, e):
        base = base_ref[e * nblk + tile * bpt]
        lo_al = (lo_ref[e * nblk + tile * bpt] // 16) * 16
        row = jnp.minimum(base + lo_al, p_rows - SLAB)
        return row - base, pl.multiple_of(row, 16)

    def fetch(tile, col, to_slot):
        for e in range(n_exp):
            _, row = window(tile, e)
            pltpu.make_async_copy(y_ref.at[pl.ds(row, SLAB), pl.ds(pl.multiple_of(col * dc, LANES), dc)],
                                  buf.at[to_slot, pl.ds(e * SLAB, SLAB), :], sem.at[to_slot]).start()

    @pl.when(step == 0)
    def _():
        fetch(tile0, 0, 0)

    pltpu.make_async_copy(y_ref.at[pl.ds(0, n_exp * SLAB), pl.ds(0, dc)], buf.at[slot], sem.at[slot]).wait()

    @pl.when(step + 1 < ni * nj)
    def _():
        nxt = step + 1
        fetch(tile0 + nxt // nj, nxt % nj, 1 - slot)

    jlane = lax.broadcasted_iota(I32, (tt, SLAB), 1)
    pieces = []
    for e in range(n_exp):
        first, _ = window(i, e)
        pieces.append(jnp.where(pos_ref[:, e:e + 1] - first == jlane, 1.0, 0.0))
    sel = jnp.concatenate(pieces, axis=1).astype(BF16)
    o_ref[...] = h_ref[...] + jnp.dot(sel, buf[slot], preferred_element_type=F32)

    for e in range(n_exp):
        first, _ = window(i, e)
        hi = hi_ref[e * nblk + i * bpt + bpt - 1]
        base = base_ref[e * nblk + i * bpt]
        pos = pos_ref[:, e:e + 1]

        def extra(c, carry):
            start = first + (c + 1) * SLAB
            row = jnp.minimum(base + start, p_rows - SLAB)
            cp = pltpu.make_async_copy(y_ref.at[pl.ds(pl.multiple_of(row, 16), SLAB),
                                                pl.ds(pl.multiple_of(j * dc, LANES), dc)], xbuf, xsem)
            cp.start()
            cp.wait()
            more = ((pos - (row - base) == jlane) & (pos >= start)).astype(BF16)
            o_ref[...] += jnp.dot(more, xbuf[...], preferred_element_type=F32)
            return carry

        lax.fori_loop(0, jnp.maximum(hi - first - 1, 0) // SLAB, extra, 0)


def combine(h, pos_t, y, lo_flat, hi_flat, base_flat, n_exp, tok0=0, ntok=None):
    n, d = h.shape
    ntok = n if ntok is None else ntok
    tt = _tile(n, 256, LANES)
    assert tok0 % tt == 0 and ntok % tt == 0
    tile0 = tok0 // tt
    dc = d
    return pl.pallas_call(
        functools.partial(_combine_kernel, n_exp=n_exp, bpt=tt // LANES, p_rows=y.shape[0], tile0=tile0),
        grid_spec=pltpu.PrefetchScalarGridSpec(
            num_scalar_prefetch=3,
            grid=(ntok // tt, d // dc),
            in_specs=[
                pl.BlockSpec((tt, dc), lambda i, j, *_: (i + tile0, j)),
                pl.BlockSpec((tt, LANES), lambda i, j, *_: (i + tile0, 0)),
                pl.BlockSpec(memory_space=pl.ANY),
            ],
            out_specs=pl.BlockSpec((tt, dc), lambda i, j, *_: (i, j)),
            scratch_shapes=[pltpu.VMEM((2, n_exp * SLAB, dc), BF16), pltpu.VMEM((SLAB, dc), BF16),
                            pltpu.SemaphoreType.DMA((2,)), pltpu.SemaphoreType.DMA(())],
        ),
        out_shape=jax.ShapeDtypeStruct((ntok, d), F32),
        compiler_params=pltpu.CompilerParams(dimension_semantics=("arbitrary", "arbitrary"),
                                             vmem_limit_bytes=VMEM_LIMIT, disable_bounds_checks=True),
        name="combine",
    )(lo_flat, hi_flat, base_flat, h, pos_t, y)


def expert_choice_ffn(h, norm_g, w_router_pad, w_gate, w_up, w_down, layer, seqs, cfg, split=None):
    n, d = h.shape
    ne = cfg.n_experts
    hn, aff = norm_proj(h, norm_g, w_router_pad, jnp.zeros((1, LANES), F32), eps=cfg.eps, softmax_cols=ne,
                        packed=True)
    aff_t = aff[:, :ne].T
    caps = [cfg.ec_capacity * s // ne for _, s in seqs]
    capt = sum(caps)
    idx_l, gate_l, pos_l, lo_l, hi_l, base_l = [], [], [], [], [], []
    row0 = 0
    for (off, s), cap in zip(seqs, caps):
        idx, gate, pos, lo = route(aff_t[:, off:off + s], cap)
        idx_l.append(idx + off)
        gate_l.append(gate)
        pos_l.append(pos)
        lo_l.append(lo)
        hi_l.append(jnp.concatenate([lo[:, 1:], jnp.full((ne, 1), cap, I32)], axis=1))
        base_l.append(jnp.broadcast_to((jnp.arange(ne, dtype=I32) * capt + row0)[:, None], lo.shape))
        row0 += cap
    gidx = jnp.concatenate(idx_l, axis=1).reshape(-1)
    gate_col = jnp.concatenate(gate_l, axis=1).reshape(-1, 1)
    pos_t = _pad_cols(jnp.concatenate(pos_l, axis=1).T)
    lo_flat, hi_flat, base_flat = (jnp.concatenate(t, axis=1).reshape(-1) for t in (lo_l, hi_l, base_l))

    xe = gather_rows(hn, gidx)
    hid = ffn_up(xe, w_gate, w_up, layer, ne)
    y = ffn_down(hid, w_down, gate_col, layer, ne)
    if split is None:
        return combine(h, pos_t, y, lo_flat, hi_flat, base_flat, ne)
    return [combine(h, pos_t, y, lo_flat, hi_flat, base_flat, ne, t0, nt) for t0, nt in split]


def _pad_cols(w, cols=LANES):
    return jnp.pad(w, [(0, 0)] * (w.ndim - 1) + [(0, cols - w.shape[-1])])


def trunk(x, seqs, out_split, p, cfg):
    n, d = x.shape
    nh = cfg.ml_heads
    depth = p["w_in"].shape[0]
    o_g = 3 * cfg.naw + 2 * cfg.qkw + 2 * cfg.vw
    w_in = p["w_in"]
    w_main = jnp.concatenate([w_in[:, :, :o_g], w_in[:, :, o_g + 4 * nh:]], axis=-1).astype(BF16)
    w_gates = _pad_cols(w_in[:, :, o_g:o_g + 4 * nh]).astype(BF16)
    b_gates = _pad_cols(p["b_gate"])[:, None, :]
    w_pa, w_pm, w_out = (p[k].astype(BF16) for k in ("w_pa", "w_pm", "w_out"))
    w_router = _pad_cols(p["w_router"]).astype(BF16)
    w_fg, w_fu, w_fd = (p[k].astype(BF16) for k in ("w_ff_gate", "w_ff_up", "w_ff_down"))
    na_bias = na_bias_table(p["rpb"], cfg)
    col_scale = jnp.concatenate([jnp.full((1, cfg.qkw), cfg.ml_qk ** -0.5, F32), jnp.ones((1, cfg.qkw), F32)], axis=1)

    for l in range(depth):
        xn, gates = norm_proj(x, p["norm1_g"][l][None], w_gates[l], b_gates[l], eps=cfg.eps)
        z = in_proj(xn, w_main, l)
        ya = neighborhood_attention(z, na_bias[l], p["q_gain"][l][None], p["k_gain"][l][None], seqs, cfg)
        mqk = conv_silu(z, p["conv_w"][l], p["conv_b"][l][None], col_scale, seqs, cfg)
        kt = mqk[:, cfg.qkw:].T
        g_dir = jnp.stack([_pad_cols(gates[:, 0:2 * nh]), _pad_cols(gates[:, 2 * nh:4 * nh])])
        pc = gate_cumsum(g_dir, cfg)
        pr = jnp.transpose(pc[:, :, :2 * nh], (0, 2, 1))
        hdir = mlstm(mqk, kt, z, pc, pr, seqs, cfg)
        ym = ml_out(hdir, z, p["ml_gain"][l].reshape(1, -1), cfg)
        merged = merge_proj(ya, ym, w_pa, w_pm, z, l, cfg)
        h = out_proj(merged, w_out, x, l)
        x = expert_choice_ffn(h, p["norm2_g"][l][None], w_router[l], w_fg, w_fu, w_fd, l, seqs, cfg,
                              split=out_split if l == depth - 1 else None)
    return x


def kernel(x_prompt, x_sample, norm1_g, norm2_g, w_in, b_gate, conv_w, conv_b, q_gain, k_gain, rpb, ml_gain,
           w_pa, w_pm, w_out, w_router, w_ff_gate, w_ff_up, w_ff_down):
    cfg = CFG
    d = x_prompt.shape[-1]
    xs = [x_prompt.reshape(-1, d), x_sample.reshape(-1, d)]
    seqs, off = [], 0
    for xx in (x_prompt, x_sample):
        for _ in range(xx.shape[0]):
            seqs.append((off, xx.shape[1]))
            off += xx.shape[1]
    p = dict(norm1_g=norm1_g, norm2_g=norm2_g, w_in=w_in, b_gate=b_gate, conv_w=conv_w, conv_b=conv_b,
             q_gain=q_gain, k_gain=k_gain, rpb=rpb, ml_gain=ml_gain, w_pa=w_pa, w_pm=w_pm, w_out=w_out,
             w_router=w_router, w_ff_gate=w_ff_gate, w_ff_up=w_ff_up, w_ff_down=w_ff_down)
    n_p, n_s = xs[0].shape[0], xs[1].shape[0]
    y_p, y_s = trunk(jnp.concatenate(xs, axis=0), seqs, [(0, n_p), (n_p, n_s)], p, cfg)
    return (y_p.reshape(x_prompt.shape), y_s.reshape(x_sample.shape))
```
